```python
import jax, jax.numpy as jnp
from jax import lax
import numpy as np

D_MODEL = 1024
BATCH = 16
SEQ = 2048
DEPTH = 1

N_META = 16
RET_HEADS = 8
RET_DK = 64
RET_DV = 128
RET_CHUNK = 128
RET_W = RET_HEADS * RET_DV
ATT_Q_HEADS = 16
ATT_KV_HEADS = 4
ATT_GROUP = ATT_Q_HEADS // ATT_KV_HEADS
ATT_DH = 64
ATT_W = ATT_Q_HEADS * ATT_DH
WINDOW = 128
ATT_BLOCK = 128
D_FF = -(-8 * D_MODEL // (3 * 256)) * 256
IN_SIZES = (RET_HEADS * RET_DK, RET_HEADS * RET_DK, RET_W, RET_W,
            ATT_Q_HEADS * ATT_DH, ATT_KV_HEADS * ATT_DH, ATT_KV_HEADS * ATT_DH,
            D_MODEL, D_MODEL)
IN_COLS = sum(IN_SIZES)
RMS_EPS = 1e-6
GN_EPS = 1e-5
NEG_INF = -1e30

kernel_name = "hybrid_retention_swa_alibi_meta_encoder"


def rmsnorm(x, g):
    x32 = x.astype(jnp.float32)
    y = x32 * lax.rsqrt(jnp.mean(x32 * x32, axis=-1, keepdims=True) + RMS_EPS)
    return (y * g.astype(jnp.float32)).astype(x.dtype)


def alibi_slopes(n_heads):
    return 2.0 ** (-8.0 * jnp.arange(1, n_heads + 1, dtype=jnp.float32) / n_heads)


def _retention_scan(q, k, v, log_gamma, strict):
    C = q.shape[3]
    idx = jnp.arange(C, dtype=jnp.float32)
    diff = idx[:, None] - idx[None, :]
    mask = (diff > 0) if strict else (diff >= 0)
    decay_intra = jnp.where(mask, jnp.exp(jnp.where(mask, diff, 0.0)[None] * log_gamma[:, None, None]), 0.0)
    q_decay = jnp.exp((idx[None, :] + 1.0) * log_gamma[:, None])[..., None]
    k_decay = jnp.exp((C - 1.0 - idx[None, :]) * log_gamma[:, None])[..., None]
    chunk_decay = jnp.exp(C * log_gamma)[:, None, None]

    def step(state, qkv):
        qc, kc, vc = qkv
        scores = jnp.einsum('bhid,bhjd->bhij', qc, kc) * decay_intra
        out = (jnp.einsum('bhij,bhje->bhie', scores, vc)
               + jnp.einsum('bhid,bhde->bhie', qc * q_decay, state))
        state = chunk_decay * state + jnp.einsum('bhjd,bhje->bhde', kc * k_decay, vc)
        return state, out

    B, H, dk, dv = q.shape[1], q.shape[2], q.shape[4], v.shape[4]
    state0 = jnp.zeros((B, H, dk, dv), jnp.float32)
    _, out = lax.scan(step, state0, (q, k, v))
    return out


def bidirectional_retention(q, k, v, g, logit_fwd, logit_bwd, gn_gain):
    B, L = q.shape[0], q.shape[1]
    pad = (-L) % RET_CHUNK
    n_chunks = (L + pad) // RET_CHUNK

    def to_chunks(t):
        t = jnp.pad(t.astype(jnp.float32), ((0, 0), (pad, 0), (0, 0), (0, 0)))
        return t.reshape(B, n_chunks, RET_CHUNK, t.shape[2], t.shape[3]).transpose(1, 0, 3, 2, 4)

    qc = to_chunks(q)
    kc = to_chunks(k) * (RET_DK ** -0.5)
    vc = to_chunks(v)
    log_gf = jax.nn.log_sigmoid(logit_fwd.astype(jnp.float32))
    log_gb = jax.nn.log_sigmoid(logit_bwd.astype(jnp.float32))
    flip = lambda t: t[::-1, :, :, ::-1]
    fwd = _retention_scan(qc, kc, vc, log_gf, False)
    bwd = flip(_retention_scan(flip(qc), flip(kc), flip(vc), log_gb, True))
    o = (fwd + bwd).transpose(1, 0, 3, 2, 4).reshape(B, n_chunks * RET_CHUNK, RET_HEADS, RET_DV)[:, pad:]
    mu = jnp.mean(o, axis=-1, keepdims=True)
    var = jnp.mean(jnp.square(o - mu), axis=-1, keepdims=True)
    o = ((o - mu) * lax.rsqrt(var + GN_EPS)).reshape(B, L, RET_W) * gn_gain.astype(jnp.float32)
    return (jax.nn.silu(g.astype(jnp.float32)) * o).astype(g.dtype)


def windowed_gqa_alibi(q, k, v, sink):
    B, L = q.shape[0], q.shape[1]
    S = L - N_META
    nb = S // ATT_BLOCK
    K, G, BLK, dh = ATT_KV_HEADS, ATT_GROUP, ATT_BLOCK, ATT_DH
    q = (q * (dh ** -0.5)).reshape(B, L, K, G, dh).transpose(0, 2, 3, 1, 4)
    k = k.transpose(0, 2, 1, 3)
    v = v.transpose(0, 2, 1, 3)
    slopes = alibi_slopes(ATT_Q_HEADS).reshape(K, G)
    sink = sink.astype(jnp.float32).reshape(K, G)
    qm, qr = q[:, :, :, :N_META], q[:, :, :, N_META:]
    km, kr = k[:, :, :N_META], k[:, :, N_META:]
    vm, vr = v[:, :, :N_META], v[:, :, N_META:]

    qb = qr.reshape(B, K, G, nb, BLK, dh)

    def band(t):
        tp = jnp.pad(t, ((0, 0), (0, 0), (BLK, BLK), (0, 0))).reshape(B, K, nb + 2, BLK, dh)
        return jnp.concatenate([tp[:, :, :-2], tp[:, :, 1:-1], tp[:, :, 2:]], axis=3)

    kb, vb = band(kr), band(vr)
    r = jnp.arange(BLK)
    j = jnp.arange(3 * BLK)
    dist = jnp.abs(j[None, :] - BLK - r[:, None])
    key_pos = jnp.arange(nb)[:, None] * BLK - BLK + j[None, :]
    valid = (dist <= WINDOW)[None] & ((key_pos >= 0) & (key_pos < S))[:, None, :]
    s_band = (jnp.einsum('bkgnqd,bknsd->bkgnqs', qb, kb).astype(jnp.float32)
              - slopes[:, :, None, None, None] * dist.astype(jnp.float32))
    s_band = jnp.where(valid, s_band, NEG_INF)
    s_meta = jnp.einsum('bkgnqd,bksd->bkgnqs', qb, km).astype(jnp.float32)
    s_sink = jnp.broadcast_to(sink[None, :, :, None, None, None], (B, K, G, nb, BLK, 1))
    p = jax.nn.softmax(jnp.concatenate([s_meta, s_band, s_sink], axis=-1), axis=-1).astype(v.dtype)
    o_real = (jnp.einsum('bkgnqs,bksd->bkgnqd', p[..., :N_META], vm)
              + jnp.einsum('bkgnqs,bknsd->bkgnqd', p[..., N_META:N_META + 3 * BLK], vb))
    o_real = o_real.reshape(B, K, G, S, dh)

    kf, vf = kr[:, :, :BLK], vr[:, :, :BLK]
    dist_m = N_META + jnp.arange(BLK)[None, :] - jnp.arange(N_META)[:, None]
    sm_meta = jnp.einsum('bkgid,bksd->bkgis', qm, km).astype(jnp.float32)
    sm_real = (jnp.einsum('bkgid,bksd->bkgis', qm, kf).astype(jnp.float32)
               - slopes[:, :, None, None] * dist_m.astype(jnp.float32))
    sm_real = jnp.where(dist_m <= WINDOW, sm_real, NEG_INF)
    sm_sink = jnp.broadcast_to(sink[None, :, :, None, None], (B, K, G, N_META, 1))
    pm = jax.nn.softmax(jnp.concatenate([sm_meta, sm_real, sm_sink], axis=-1), axis=-1).astype(v.dtype)
    o_meta = (jnp.einsum('bkgis,bksd->bkgid', pm[..., :N_META], vm)
              + jnp.einsum('bkgis,bksd->bkgid', pm[..., N_META:N_META + BLK], vf))

    o = jnp.concatenate([o_meta, o_real], axis=3)
    return o.transpose(0, 3, 1, 2, 4).reshape(B, L, ATT_W)


def hybrid_layer(x, w_in, ret_logit_f, ret_logit_b, ret_gn, attn_sink,
                 w_branch_ret, w_branch_att, w_out, norm_mix, norm_ffn, w_gate_up, w_down):
    B, L, _ = x.shape
    h = rmsnorm(x, norm_mix)
    proj = h @ w_in
    split_at = [int(c) for c in np.cumsum(IN_SIZES)[:-1]]
    rq, rk, rv, rg, aq, ak, av, ga, gb = jnp.split(proj, split_at, axis=-1)
    ret = bidirectional_retention(rq.reshape(B, L, RET_HEADS, RET_DK), rk.reshape(B, L, RET_HEADS, RET_DK),
                                  rv.reshape(B, L, RET_HEADS, RET_DV), rg, ret_logit_f, ret_logit_b, ret_gn)
    att = windowed_gqa_alibi(aq.reshape(B, L, ATT_Q_HEADS, ATT_DH), ak.reshape(B, L, ATT_KV_HEADS, ATT_DH),
                             av.reshape(B, L, ATT_KV_HEADS, ATT_DH), attn_sink)
    merged = jax.nn.sigmoid(ga) * (ret @ w_branch_ret) + jax.nn.sigmoid(gb) * (att @ w_branch_att)
    x = x + merged @ w_out
    h = rmsnorm(x, norm_ffn)
    a, b = jnp.split(h @ w_gate_up, 2, axis=-1)
    return x + (jax.nn.silu(a) * b) @ w_down


def setup_inputs(seed: int = 0) -> dict:
    key = jax.random.key(seed)
    ks = jax.random.split(key, 16)
    f32 = jnp.float32
    nrm = lambda k, shape, fan_in: jax.random.normal(k, shape, f32) * (fan_in ** -0.5)
    base = 1.0 - 2.0 ** (-5.0 - jnp.arange(RET_HEADS, dtype=f32))
    base_logit = jnp.log(base) - jnp.log1p(-base)
    return {
        "x": jax.random.normal(ks[0], (BATCH, SEQ, D_MODEL), f32),
        "meta_tokens": jax.random.normal(ks[1], (N_META, D_MODEL), f32),
        "w_in": nrm(ks[2], (DEPTH, D_MODEL, IN_COLS), D_MODEL),
        "ret_decay_logit_fwd": base_logit[None] + 0.05 * jax.random.normal(ks[3], (DEPTH, RET_HEADS), f32),
        "ret_decay_logit_bwd": base_logit[None] + 0.05 * jax.random.normal(ks[4], (DEPTH, RET_HEADS), f32),
        "ret_gn_gain": 1.0 + 0.05 * jax.random.normal(ks[5], (DEPTH, RET_W), f32),
        "attn_sink": 0.5 * jax.random.normal(ks[6], (DEPTH, ATT_Q_HEADS), f32),
        "w_branch_ret": nrm(ks[7], (DEPTH, RET_W, D_MODEL), RET_W),
        "w_branch_att": nrm(ks[8], (DEPTH, ATT_W, D_MODEL), ATT_W),
        "w_out": nrm(ks[9], (DEPTH, D_MODEL, D_MODEL), D_MODEL),
        "norm_mix": 1.0 + 0.05 * jax.random.normal(ks[10], (DEPTH, D_MODEL), f32),
        "norm_ffn": 1.0 + 0.05 * jax.random.normal(ks[11], (DEPTH, D_MODEL), f32),
        "w_gate_up": nrm(ks[12], (DEPTH, D_MODEL, 2 * D_FF), D_MODEL),
        "w_down": nrm(ks[13], (DEPTH, D_FF, D_MODEL), D_FF),
        "norm_final": 1.0 + 0.05 * jax.random.normal(ks[14], (D_MODEL,), f32),
    }


def reference(x, meta_tokens, w_in, ret_decay_logit_fwd, ret_decay_logit_bwd, ret_gn_gain, attn_sink,
              w_branch_ret, w_branch_att, w_out, norm_mix, norm_ffn, w_gate_up, w_down, norm_final):
    B = x.shape[0]
    meta = jnp.broadcast_to(meta_tokens.astype(x.dtype)[None], (B, N_META, x.shape[2]))
    h = jnp.concatenate([meta, x], axis=1)
    for layer in range(DEPTH):
        h = hybrid_layer(h, w_in[layer], ret_decay_logit_fwd[layer], ret_decay_logit_bwd[layer],
                         ret_gn_gain[layer], attn_sink[layer], w_branch_ret[layer], w_branch_att[layer],
                         w_out[layer], norm_mix[layer], norm_ffn[layer], w_gate_up[layer], w_down[layer])
    return rmsnorm(h, norm_final)[:, N_META:]
```

```python
import functools

import numpy as np
import jax
import jax.numpy as jnp
from jax import lax
from jax.experimental import pallas as pl
from jax.experimental.pallas import tpu as pltpu

D_MODEL = 1024
N_META = 16
RET_HEADS = 8
RET_DK = 64
RET_DV = 128
CHUNK = 128
RET_W = RET_HEADS * RET_DV
ATT_Q_HEADS = 16
ATT_KV_HEADS = 4
ATT_GROUP = ATT_Q_HEADS // ATT_KV_HEADS
ATT_DH = 64
ATT_W = ATT_Q_HEADS * ATT_DH
WINDOW = 128
BLK = 128
D_FF = 2816
IN_SIZES = (512, 512, 1024, 1024, 1024, 256, 256, 1024, 1024)
IN_COLS = sum(IN_SIZES)
SRC_RQ, SRC_RK, SRC_RV, SRC_RG, SRC_AQ, SRC_AK, SRC_AV, SRC_GA, SRC_GB = (
    int(c) for c in np.cumsum((0,) + IN_SIZES[:-1]))
OFF_RQ, OFF_RK, OFF_RV, OFF_RG, OFF_AQ, OFF_GA, OFF_GB, OFF_AK, OFF_AV = (
    0, 512, 1024, 2048, 3072, 4096, 5120, 6144, 6400)
RMS_EPS = 1e-6
GN_EPS = 1e-5
NEG_INF = -1e30

LANES = 128
VMEM_LIMIT = 56 * 1024 * 1024

F32 = jnp.float32
BF16 = jnp.bfloat16


def _sigmoid(x):
    return 1.0 / (1.0 + jnp.exp(-x))


def _dot(a, b):
    return jnp.dot(a, b, preferred_element_type=F32)


def _dot_nt(a, b):
    return lax.dot_general(a, b, (((1,), (1,)), ((), ())), preferred_element_type=F32)


def _dot_tn(a, b):
    return lax.dot_general(a, b, (((0,), (0,)), ((), ())), preferred_element_type=F32)


IN_CHUNK = 512


def _inproj_kernel(x_ref, g_ref, w_ref, o_ref, h_ref):
    x = x_ref[...]
    ms = jnp.mean(x * x, axis=-1, keepdims=True)
    h_ref[...] = (x * lax.rsqrt(ms + RMS_EPS) * g_ref[...]).astype(BF16)
    for c in range(0, IN_COLS, IN_CHUNK):
        o_ref[:, c:c + IN_CHUNK] = _dot(h_ref[...], w_ref[:, c:c + IN_CHUNK]).astype(BF16)


def _inproj_call(x2d, gain, w_bf16, tm):
    rows = x2d.shape[0]
    return pl.pallas_call(
        _inproj_kernel,
        out_shape=jax.ShapeDtypeStruct((rows, IN_COLS), BF16),
        grid=(rows // tm,),
        in_specs=[
            pl.BlockSpec((tm, D_MODEL), lambda i: (i, 0)),
            pl.BlockSpec((1, D_MODEL), lambda i: (0, 0)),
            pl.BlockSpec((D_MODEL, IN_COLS), lambda i: (0, 0), pipeline_mode=pl.Buffered(1)),
        ],
        out_specs=pl.BlockSpec((tm, IN_COLS), lambda i: (i, 0)),
        scratch_shapes=[pltpu.VMEM((tm, D_MODEL), BF16)],
        compiler_params=pltpu.CompilerParams(
            dimension_semantics=("parallel",), vmem_limit_bytes=VMEM_LIMIT),
        name="inproj",
    )(x2d, gain, w_bf16)


def _retention_kernel(lg_ref, q_ref, k_ref, v_ref, g_ref, km_ref, vm_ref, gain_ref,
                      o_ref, sb_ref, *, n_chunks):
    pair = pl.program_id(1)
    row = lax.broadcasted_iota(jnp.int32, (CHUNK, LANES), 0).astype(F32)
    col = lax.broadcasted_iota(jnp.int32, (CHUNK, LANES), 1).astype(F32)
    lane = lax.broadcasted_iota(jnp.int32, (CHUNK, LANES), 1)
    mrow = lax.broadcasted_iota(jnp.int32, (N_META, LANES), 0).astype(F32)
    mlane = lax.broadcasted_iota(jnp.int32, (N_META, LANES), 1)
    k_scale = RET_DK ** -0.5

    tabs = []
    for hl in range(2):
        lgf = lg_ref[0, 2 * pair + hl]
        lgb = lg_ref[1, 2 * pair + hl]
        mine = (lane // RET_DK) == hl
        zero = jnp.zeros((CHUNK, LANES), F32)
        diff = row - col
        tabs.append(dict(
            sel=jnp.where(mine, 1.0, 0.0),
            qf=jnp.where(mine, jnp.exp((row + 1.0) * lgf), zero),
            qb=jnp.where(mine, jnp.exp((CHUNK - row) * lgb), zero),
            kf=jnp.where(mine, jnp.exp((CHUNK - 1.0 - row) * lgf) * k_scale, zero),
            kb=jnp.where(mine, jnp.exp(row * lgb) * k_scale, zero),
            intra=jnp.where(diff >= 0, jnp.exp(jnp.maximum(diff, 0.0) * lgf),
                            jnp.exp(jnp.maximum(-diff, 0.0) * lgb)),
            cf=jnp.exp(jnp.zeros((1, LANES), F32) + CHUNK * lgf),
            cb=jnp.exp(jnp.zeros((1, LANES), F32) + CHUNK * lgb),
            km=jnp.where((mlane // RET_DK) == hl,
                         jnp.exp((N_META - 1.0 - mrow) * lgf) * k_scale, 0.0),
        ))

    def chunk(ref, c, width):
        return ref[pl.ds(pl.multiple_of(c * CHUNK, CHUNK), CHUNK), :]

    sb_ref[n_chunks - 1] = jnp.zeros((2, LANES, RET_DV), F32)

    def bwd_body(t, carry):
        c = n_chunks - 1 - t
        kc = chunk(k_ref, c, LANES).astype(F32)
        vc = chunk(v_ref, c, 2 * RET_DV)
        for hl in range(2):
            kb = (kc * tabs[hl]["kb"]).astype(BF16)
            upd = _dot_tn(kb, vc[:, hl * RET_DV:(hl + 1) * RET_DV])
            sb_ref[c - 1, hl] = sb_ref[c, hl] * tabs[hl]["cb"] + upd
        return carry

    lax.fori_loop(0, n_chunks - 1, bwd_body, 0)

    km = km_ref[...].astype(F32)
    vm = vm_ref[...]
    sf0 = tuple(
        _dot_tn((km * tabs[hl]["km"]).astype(BF16), vm[:, hl * RET_DV:(hl + 1) * RET_DV])
        for hl in range(2))

    def fwd_body(c, sf):
        qc = chunk(q_ref, c, LANES).astype(F32)
        kc = chunk(k_ref, c, LANES).astype(F32)
        vc = chunk(v_ref, c, 2 * RET_DV)
        gc = chunk(g_ref, c, 2 * RET_DV).astype(F32)
        k_in = (kc * k_scale).astype(BF16)
        new_sf = []
        outs = []
        for hl in range(2):
            t = tabs[hl]
            vh = vc[:, hl * RET_DV:(hl + 1) * RET_DV]
            s = _dot_nt((qc * t["sel"]).astype(BF16), k_in) * t["intra"]
            q_cross = jnp.concatenate([(qc * t["qf"]).astype(BF16),
                                       (qc * t["qb"]).astype(BF16)], axis=1)
            state = jnp.concatenate([sf[hl], sb_ref[c, hl]], axis=0).astype(BF16)
            o = _dot(s.astype(BF16), vh) + _dot(q_cross, state)
            mu = jnp.mean(o, axis=-1, keepdims=True)
            d = o - mu
            var = jnp.mean(d * d, axis=-1, keepdims=True)
            on = d * lax.rsqrt(var + GN_EPS) * gain_ref[:, hl * RET_DV:(hl + 1) * RET_DV]
            gh = gc[:, hl * RET_DV:(hl + 1) * RET_DV]
            outs.append((gh * _sigmoid(gh) * on).astype(BF16))
            new_sf.append(sf[hl] * t["cf"] + _dot_tn((kc * t["kf"]).astype(BF16), vh))
        o_ref[pl.ds(pl.multiple_of(c * CHUNK, CHUNK), CHUNK), :] = jnp.concatenate(outs, axis=1)
        return tuple(new_sf)

    lax.fori_loop(0, n_chunks, fwd_body, sf0)


def _retention_call(log_gammas, proj, proj_meta, gn_gain, batch, seq):
    n_chunks = seq // CHUNK
    n_pairs = RET_HEADS // 2
    kernel = functools.partial(_retention_kernel, n_chunks=n_chunks)
    qb, kb = OFF_RQ // LANES, OFF_RK // LANES
    vb, gb = OFF_RV // (2 * RET_DV), OFF_RG // (2 * RET_DV)
    return pl.pallas_call(
        kernel,
        out_shape=jax.ShapeDtypeStruct((batch, seq, RET_W), BF16),
        grid=(batch, n_pairs),
        in_specs=[
            pl.BlockSpec(memory_space=pltpu.SMEM),
            pl.BlockSpec((None, seq, LANES), lambda b, p: (b, 0, qb + p)),
            pl.BlockSpec((None, seq, LANES), lambda b, p: (b, 0, kb + p)),
            pl.BlockSpec((None, seq, 2 * RET_DV), lambda b, p: (b, 0, vb + p)),
            pl.BlockSpec((None, seq, 2 * RET_DV), lambda b, p: (b, 0, gb + p)),
            pl.BlockSpec((N_META, LANES), lambda b, p: (0, kb + p)),
            pl.BlockSpec((N_META, 2 * RET_DV), lambda b, p: (0, vb + p)),
            pl.BlockSpec((1, 2 * RET_DV), lambda b, p: (0, p)),
        ],
        out_specs=pl.BlockSpec((None, seq, 2 * RET_DV), lambda b, p: (b, 0, p)),
        scratch_shapes=[pltpu.VMEM((n_chunks, 2, LANES, RET_DV), F32)],
        compiler_params=pltpu.CompilerParams(
            dimension_semantics=("parallel", "parallel"), vmem_limit_bytes=VMEM_LIMIT),
        name="retention",
    )(log_gammas, proj, proj, proj, proj, proj_meta, proj_meta, gn_gain)


def _att_head_perm():
    perm = []
    for j in range(ATT_KV_HEADS // 2):
        for g in range(ATT_GROUP):
            for half in range(2):
                head = (2 * j + half) * ATT_GROUP + g
                perm.extend(range(head * ATT_DH, (head + 1) * ATT_DH))
    return np.asarray(perm, dtype=np.int32)


ATT_PERM = _att_head_perm()
BAND = 3 * BLK
N_KEYS = BAND + BLK


def _attention_kernel(sink_ref, q_ref, k_ref, v_ref, km_ref, vm_ref, o_ref, *, seq):
    n = pl.program_id(1)
    start = jnp.clip((n - 1) * BLK, 0, seq - BAND)
    start = pl.multiple_of(start, BLK)
    off = n * BLK - start

    r = lax.broadcasted_iota(jnp.int32, (BLK, N_KEYS), 0)
    j = lax.broadcasted_iota(jnp.int32, (BLK, N_KEYS), 1)
    dist = jnp.abs(j - off - r)
    band_ok = (j < BAND) & (dist <= WINDOW)
    meta_ok = (j >= BAND) & (j < BAND + N_META)
    nd = jnp.where(band_ok, -dist.astype(F32), jnp.where(meta_ok, 0.0, NEG_INF))

    lane = lax.broadcasted_iota(jnp.int32, (BLK, LANES), 1)
    q_scale = ATT_DH ** -0.5

    for jp in range(ATT_KV_HEADS // 2):
        ksl = slice(jp * LANES, (jp + 1) * LANES)
        k_all = jnp.concatenate([k_ref[pl.ds(start, BAND), ksl], km_ref[:, ksl]], axis=0)
        v_all = jnp.concatenate([v_ref[pl.ds(start, BAND), ksl], vm_ref[:, ksl]], axis=0)
        halves = []
        for half in range(2):
            sel = jnp.where((lane // ATT_DH) == half, q_scale, 0.0)
            q_rows = []
            for g in range(ATT_GROUP):
                grp = jp * ATT_GROUP + g
                qg = q_ref[:, grp * LANES:(grp + 1) * LANES].astype(F32)
                q_rows.append((qg * sel).astype(BF16))
            s = _dot_nt(jnp.concatenate(q_rows, axis=0), k_all)
            outs = []
            for g in range(ATT_GROUP):
                head = (2 * jp + half) * ATT_GROUP + g
                slope = 2.0 ** (-8.0 * (head + 1) / ATT_Q_HEADS)
                sink = sink_ref[head]
                sg = s[g * BLK:(g + 1) * BLK] + slope * nd
                m = jnp.maximum(jnp.max(sg, axis=-1, keepdims=True), sink)
                p = jnp.exp(sg - m)
                denom = jnp.sum(p, axis=-1, keepdims=True) + jnp.exp(sink - m)
                outs.append(_dot(p.astype(BF16), v_all) / denom)
            halves.append(outs)
        for g in range(ATT_GROUP):
            grp = jp * ATT_GROUP + g
            merged = jnp.where((lane // ATT_DH) == 0, halves[0][g], halves[1][g])
            o_ref[:, grp * LANES:(grp + 1) * LANES] = merged.astype(BF16)


def _attention_call(sink, proj, k_meta, v_meta, batch, seq):
    kernel = functools.partial(_attention_kernel, seq=seq)
    kv_w = ATT_KV_HEADS * ATT_DH
    return pl.pallas_call(
        kernel,
        out_shape=jax.ShapeDtypeStruct((batch, seq, ATT_W), BF16),
        grid=(batch, seq // BLK),
        in_specs=[
            pl.BlockSpec(memory_space=pltpu.SMEM),
            pl.BlockSpec((None, BLK, ATT_W), lambda b, n: (b, n, OFF_AQ // ATT_W)),
            pl.BlockSpec((None, seq, kv_w), lambda b, n: (b, 0, OFF_AK // kv_w)),
            pl.BlockSpec((None, seq, kv_w), lambda b, n: (b, 0, OFF_AV // kv_w)),
            pl.BlockSpec((BLK, kv_w), lambda b, n: (0, 0)),
            pl.BlockSpec((BLK, kv_w), lambda b, n: (0, 0)),
        ],
        out_specs=pl.BlockSpec((None, BLK, ATT_W), lambda b, n: (b, n, 0)),
        compiler_params=pltpu.CompilerParams(
            dimension_semantics=("parallel", "arbitrary"), vmem_limit_bytes=VMEM_LIMIT),
        name="attention",
    )(sink, proj, proj, proj, k_meta, v_meta)


FF_CHUNK = 256


def _mixffn_kernel(x_ref, ret_ref, att_ref, ga_ref, gb_ref, wr_ref, wa_ref, wo_ref,
                   nf_ref, wg_ref, wu_ref, wd_ref, nfin_ref, o_ref, h_ref, act_ref):
    ga = ga_ref[...].astype(F32)
    gb = gb_ref[...].astype(F32)
    merged = (_sigmoid(ga) * _dot(ret_ref[...], wr_ref[...])
              + _sigmoid(gb) * _dot(att_ref[...], wa_ref[...]))
    x = x_ref[...] + _dot(merged.astype(BF16), wo_ref[...])
    ms = jnp.mean(x * x, axis=-1, keepdims=True)
    h_ref[...] = (x * lax.rsqrt(ms + RMS_EPS) * nf_ref[...]).astype(BF16)
    for c in range(0, D_FF, FF_CHUNK):
        a = _dot(h_ref[...], wg_ref[:, c:c + FF_CHUNK])
        b = _dot(h_ref[...], wu_ref[:, c:c + FF_CHUNK])
        act_ref[:, c:c + FF_CHUNK] = (a * _sigmoid(a) * b).astype(BF16)
    y = x + _dot(act_ref[...], wd_ref[...])
    ms = jnp.mean(y * y, axis=-1, keepdims=True)
    o_ref[...] = y * lax.rsqrt(ms + RMS_EPS) * nfin_ref[...]


def _mixffn_call(x2d, ret, att, proj, wr, wa, wo, norm_ffn, wg, wu, wd, norm_final, tm):
    rows = x2d.shape[0]
    const = lambda shape: pl.BlockSpec(shape, lambda i: (0, 0), pipeline_mode=pl.Buffered(1))
    row_tile = lambda col: pl.BlockSpec((tm, D_MODEL), lambda i: (i, col))
    return pl.pallas_call(
        _mixffn_kernel,
        out_shape=jax.ShapeDtypeStruct((rows, D_MODEL), F32),
        grid=(rows // tm,),
        in_specs=[
            row_tile(0), row_tile(0), row_tile(0),
            row_tile(OFF_GA // D_MODEL), row_tile(OFF_GB // D_MODEL),
            const((RET_W, D_MODEL)), const((ATT_W, D_MODEL)), const((D_MODEL, D_MODEL)),
            const((1, D_MODEL)),
            const((D_MODEL, D_FF)), const((D_MODEL, D_FF)), const((D_FF, D_MODEL)),
            const((1, D_MODEL)),
        ],
        out_specs=row_tile(0),
        scratch_shapes=[pltpu.VMEM((tm, D_MODEL), BF16), pltpu.VMEM((tm, D_FF), BF16)],
        compiler_params=pltpu.CompilerParams(
            dimension_semantics=("parallel",), vmem_limit_bytes=VMEM_LIMIT),
        name="mixffn",
    )(x2d, ret, att, proj, proj, wr, wa, wo, norm_ffn, wg, wu, wd, norm_final)


def kernel(x, meta_tokens, w_in, ret_decay_logit_fwd, ret_decay_logit_bwd, ret_gn_gain, attn_sink,
           w_branch_ret, w_branch_att, w_out, norm_mix, norm_ffn, w_gate_up, w_down, norm_final):
    batch, seq, d = x.shape
    assert d == D_MODEL and seq % CHUNK == 0 and w_in.shape[0] == 1
    perm = jnp.asarray(ATT_PERM)

    w_in0 = w_in[0]
    w_in_b = jnp.concatenate(
        [w_in0[:, :SRC_AQ], w_in0[:, SRC_AQ:SRC_AK][:, perm], w_in0[:, SRC_GA:],
         w_in0[:, SRC_AK:SRC_GA]], axis=1).astype(BF16)
    wr = w_branch_ret[0].astype(BF16)
    wa = w_branch_att[0][perm, :].astype(BF16)
    wo = w_out[0].astype(BF16)
    wg = w_gate_up[0][:, :D_FF].astype(BF16)
    wu = w_gate_up[0][:, D_FF:].astype(BF16)
    wd = w_down[0].astype(BF16)
    nmix = norm_mix[0].reshape(1, D_MODEL)
    nffn = norm_ffn[0].reshape(1, D_MODEL)
    nfin = norm_final.reshape(1, D_MODEL)

    x2d = x.reshape(batch * seq, D_MODEL)
    proj = _inproj_call(x2d, nmix, w_in_b, 512)
    proj_meta = _inproj_call(meta_tokens.astype(x.dtype), nmix, w_in_b, N_META)
    proj3 = proj.reshape(batch, seq, IN_COLS)

    log_gammas = jnp.stack([jax.nn.log_sigmoid(ret_decay_logit_fwd[0].astype(F32)),
                            jax.nn.log_sigmoid(ret_decay_logit_bwd[0].astype(F32))])
    ret = _retention_call(log_gammas, proj3, proj_meta, ret_gn_gain[0].reshape(1, RET_W), batch, seq)

    kv_w = ATT_KV_HEADS * ATT_DH
    pad = ((0, BLK - N_META), (0, 0))
    k_meta = jnp.pad(proj_meta[:, OFF_AK:OFF_AK + kv_w], pad)
    v_meta = jnp.pad(proj_meta[:, OFF_AV:OFF_AV + kv_w], pad)
    att = _attention_call(attn_sink[0].astype(F32), proj3, k_meta, v_meta, batch, seq)

    out = _mixffn_call(x2d, ret.reshape(batch * seq, RET_W), att.reshape(batch * seq, ATT_W), proj,
                       wr, wa, wo, nffn, wg, wu, wd, nfin, 512)
    return out.reshape(batch, seq, D_MODEL)
```

```python
import functools

import numpy as np
import jax
import jax.numpy as jnp
from jax import lax
from jax.experimental import pallas as pl
from jax.experimental.pallas import tpu as pltpu

D_MODEL = 1024
N_META = 16
RET_HEADS = 8
RET_DK = 64
RET_DV = 128
CHUNK = 128
RET_W = RET_HEADS * RET_DV
ATT_Q_HEADS = 16
ATT_KV_HEADS = 4
ATT_GROUP = ATT_Q_HEADS // ATT_KV_HEADS
ATT_DH = 64
ATT_W = ATT_Q_HEADS * ATT_DH
WINDOW = 128
BLK = 128
D_FF = 2816
IN_SIZES = (512, 512, 1024, 1024, 1024, 256, 256, 1024, 1024)
IN_COLS = sum(IN_SIZES)
SRC_RQ, SRC_RK, SRC_RV, SRC_RG, SRC_AQ, SRC_AK, SRC_AV, SRC_GA, SRC_GB = (
    int(c) for c in np.cumsum((0,) + IN_SIZES[:-1]))
OFF_RQ, OFF_RK, OFF_RV, OFF_RG, OFF_AQ, OFF_GA, OFF_GB, OFF_AK, OFF_AV = (
    0, 512, 1024, 2048, 3072, 4096, 5120, 6144, 6400)
RMS_EPS = 1e-6
GN_EPS = 1e-5
NEG_INF = -1e30
LOG2E = 1.4426950408889634
ATT_Q_SCALE = ATT_DH ** -0.5 * LOG2E

LANES = 128
VMEM_LIMIT = 56 * 1024 * 1024

F32 = jnp.float32
BF16 = jnp.bfloat16


def _sigmoid(x):
    return 1.0 / (1.0 + jnp.exp(-x))


def _dot(a, b):
    return jnp.dot(a, b, preferred_element_type=F32)


def _dot_nt(a, b):
    return lax.dot_general(a, b, (((1,), (1,)), ((), ())), preferred_element_type=F32)


def _dot_tn(a, b):
    return lax.dot_general(a, b, (((0,), (0,)), ((), ())), preferred_element_type=F32)


IN_CHUNK = 512


def _inproj_kernel(x_ref, g_ref, w_ref, o_ref, h_ref):
    x = x_ref[...]
    ms = jnp.mean(x * x, axis=-1, keepdims=True)
    h_ref[...] = (x * lax.rsqrt(ms + RMS_EPS) * g_ref[...]).astype(BF16)
    for c in range(0, IN_COLS, IN_CHUNK):
        acc = _dot(h_ref[...], w_ref[:, c:c + IN_CHUNK])
        if OFF_AQ <= c < OFF_AQ + ATT_W:
            acc = acc * ATT_Q_SCALE
        o_ref[:, c:c + IN_CHUNK] = acc.astype(BF16)


def _inproj_call(x2d, gain, w_bf16, tm):
    rows = x2d.shape[0]
    return pl.pallas_call(
        _inproj_kernel,
        out_shape=jax.ShapeDtypeStruct((rows, IN_COLS), BF16),
        grid=(rows // tm,),
        in_specs=[
            pl.BlockSpec((tm, D_MODEL), lambda i: (i, 0)),
            pl.BlockSpec((1, D_MODEL), lambda i: (0, 0)),
            pl.BlockSpec((D_MODEL, IN_COLS), lambda i: (0, 0), pipeline_mode=pl.Buffered(1)),
        ],
        out_specs=pl.BlockSpec((tm, IN_COLS), lambda i: (i, 0)),
        scratch_shapes=[pltpu.VMEM((tm, D_MODEL), BF16)],
        compiler_params=pltpu.CompilerParams(
            dimension_semantics=("parallel",), vmem_limit_bytes=VMEM_LIMIT),
        name="inproj",
    )(x2d, gain, w_bf16)


RET_C = 256


def _retention_kernel(lg_ref, q_ref, k_ref, v_ref, g_ref, km_ref, vm_ref, gain_ref, o_ref,
                      tq_ref, tk_ref, td_ref, tc_ref, tm_ref, u_ref, s_ref, *, n_chunks):
    pair = pl.program_id(0)
    C = RET_C
    k_scale = RET_DK ** -0.5

    @pl.when(pl.program_id(1) == 0)
    def _build_tables():
        row = lax.broadcasted_iota(jnp.int32, (C, LANES), 0).astype(F32)
        lane = lax.broadcasted_iota(jnp.int32, (C, LANES), 1)
        pos_t = lax.broadcasted_iota(jnp.int32, (LANES, C), 1).astype(F32)
        lane_t = lax.broadcasted_iota(jnp.int32, (LANES, C), 0)
        diff = (lax.broadcasted_iota(jnp.int32, (C, C), 0)
                - lax.broadcasted_iota(jnp.int32, (C, C), 1)).astype(F32)
        mrow = lax.broadcasted_iota(jnp.int32, (N_META, LANES), 0).astype(F32)
        mlane = lax.broadcasted_iota(jnp.int32, (N_META, LANES), 1)
        ones = jnp.ones((8, LANES), F32)
        for hl in range(2):
            lgf = lg_ref[0, 2 * pair + hl]
            lgb = lg_ref[1, 2 * pair + hl]
            mine = (lane // RET_DK) == hl
            mine_t = (lane_t // RET_DK) == hl
            tq_ref[hl, 0] = jnp.where(mine, k_scale, 0.0)
            tq_ref[hl, 1] = jnp.where(mine, jnp.exp((row + 1.0) * lgf), 0.0)
            tq_ref[hl, 2] = jnp.where(mine, jnp.exp((C - row) * lgb), 0.0)
            tk_ref[hl, 0] = jnp.where(mine_t, jnp.exp((C - 1.0 - pos_t) * lgf) * k_scale, 0.0)
            tk_ref[hl, 1] = jnp.where(mine_t, jnp.exp(pos_t * lgb) * k_scale, 0.0)
            td_ref[hl] = jnp.where(diff >= 0, jnp.exp(jnp.maximum(diff, 0.0) * lgf),
                                   jnp.exp(jnp.maximum(-diff, 0.0) * lgb))
            tc_ref[hl, 0] = jnp.exp(ones * (C * lgf))
            tc_ref[hl, 1] = jnp.exp(ones * (C * lgb))
            tm_ref[hl] = jnp.where((mlane // RET_DK) == hl,
                                   jnp.exp((N_META - 1.0 - mrow) * lgf) * k_scale, 0.0)

    def head(x, hl):
        return x[:, hl * RET_DV:(hl + 1) * RET_DV]

    for c in range(n_chunks):
        rows = slice(c * C, (c + 1) * C)
        k_t = k_ref[rows, :].astype(F32).T
        vc = v_ref[rows, :]
        for hl in range(2):
            lhs = jnp.concatenate([k_t * tk_ref[hl, 0], k_t * tk_ref[hl, 1]], axis=0)
            u_ref[c, hl] = _dot(lhs.astype(BF16), head(vc, hl))

    km = km_ref[...].astype(F32)
    vm = vm_ref[...]
    for hl in range(2):
        cf = tc_ref[hl, 0][0:1, :]
        cb = tc_ref[hl, 1][0:1, :]
        sf = _dot_tn((km * tm_ref[hl]).astype(BF16), head(vm, hl))
        for c in range(n_chunks):
            s_ref[c, hl, 0:LANES, :] = sf.astype(BF16)
            if c + 1 < n_chunks:
                sf = sf * cf + u_ref[c, hl, 0:LANES, :]
        sb = jnp.zeros((LANES, RET_DV), F32)
        for c in reversed(range(n_chunks)):
            s_ref[c, hl, LANES:2 * LANES, :] = sb.astype(BF16)
            if c > 0:
                sb = sb * cb + u_ref[c, hl, LANES:2 * LANES, :]

    for c in range(n_chunks):
        rows = slice(c * C, (c + 1) * C)
        qc = q_ref[rows, :].astype(F32)
        kc = k_ref[rows, :]
        vc = v_ref[rows, :]
        gc = g_ref[rows, :].astype(F32)
        for hl in range(2):
            s = _dot_nt((qc * tq_ref[hl, 0]).astype(BF16), kc) * td_ref[hl]
            q_cross = jnp.concatenate([(qc * tq_ref[hl, 1]).astype(BF16),
                                       (qc * tq_ref[hl, 2]).astype(BF16)], axis=1)
            o = _dot(s.astype(BF16), head(vc, hl)) + _dot(q_cross, s_ref[c, hl])
            mu = jnp.mean(o, axis=-1, keepdims=True)
            d = o - mu
            var = jnp.mean(d * d, axis=-1, keepdims=True)
            on = d * lax.rsqrt(var + GN_EPS) * head(gain_ref[...], hl)
            gh = head(gc, hl)
            o_ref[rows, hl * RET_DV:(hl + 1) * RET_DV] = (gh * _sigmoid(gh) * on).astype(BF16)


def _retention_call(log_gammas, proj, proj_meta, gn_gain, batch, seq):
    assert seq % RET_C == 0
    n_chunks = seq // RET_C
    n_pairs = RET_HEADS // 2
    kernel = functools.partial(_retention_kernel, n_chunks=n_chunks)
    qb, kb = OFF_RQ // LANES, OFF_RK // LANES
    vb, gb = OFF_RV // (2 * RET_DV), OFF_RG // (2 * RET_DV)
    return pl.pallas_call(
        kernel,
        out_shape=jax.ShapeDtypeStruct((batch, seq, RET_W), BF16),
        grid=(n_pairs, batch),
        in_specs=[
            pl.BlockSpec(memory_space=pltpu.SMEM),
            pl.BlockSpec((None, seq, LANES), lambda p, b: (b, 0, qb + p)),
            pl.BlockSpec((None, seq, LANES), lambda p, b: (b, 0, kb + p)),
            pl.BlockSpec((None, seq, 2 * RET_DV), lambda p, b: (b, 0, vb + p)),
            pl.BlockSpec((None, seq, 2 * RET_DV), lambda p, b: (b, 0, gb + p)),
            pl.BlockSpec((N_META, LANES), lambda p, b: (0, kb + p)),
            pl.BlockSpec((N_META, 2 * RET_DV), lambda p, b: (0, vb + p)),
            pl.BlockSpec((1, 2 * RET_DV), lambda p, b: (0, p)),
        ],
        out_specs=pl.BlockSpec((None, seq, 2 * RET_DV), lambda p, b: (b, 0, p)),
        scratch_shapes=[
            pltpu.VMEM((2, 3, RET_C, LANES), F32),
            pltpu.VMEM((2, 2, LANES, RET_C), F32),
            pltpu.VMEM((2, RET_C, RET_C), F32),
            pltpu.VMEM((2, 2, 8, LANES), F32),
            pltpu.VMEM((2, N_META, LANES), F32),
            pltpu.VMEM((n_chunks, 2, 2 * LANES, RET_DV), F32),
            pltpu.VMEM((n_chunks, 2, 2 * LANES, RET_DV), BF16),
        ],
        compiler_params=pltpu.CompilerParams(
            dimension_semantics=("arbitrary", "arbitrary"), vmem_limit_bytes=VMEM_LIMIT),
        name="retention",
    )(log_gammas, proj, proj, proj, proj, proj_meta, proj_meta, gn_gain)


def _att_head_perm():
    perm = []
    for j in range(ATT_KV_HEADS // 2):
        for g in range(ATT_GROUP):
            for half in range(2):
                head = (2 * j + half) * ATT_GROUP + g
                perm.extend(range(head * ATT_DH, (head + 1) * ATT_DH))
    return np.asarray(perm, dtype=np.int32)


ATT_PERM = _att_head_perm()
QB = 64
BAND = QB + 2 * WINDOW
META_PAD = 64
N_KEYS = BAND + META_PAD
ATT_ROWS = 256


N_OFFSETS = 2 * WINDOW // QB + 1
SINK_COL = BAND + N_META
HB_UNROLL = 4


def _attention_kernel(sink_ref, q_ref, k_ref, v_ref, km_ref, vm_ref, o_ref, bias_ref, *, seq):
    n = pl.program_id(1)

    @pl.when((pl.program_id(0) == 0) & (n == 0))
    def _build_bias():
        r = lax.broadcasted_iota(jnp.int32, (QB, N_KEYS), 0)
        j = lax.broadcasted_iota(jnp.int32, (QB, N_KEYS), 1)
        for t in range(N_OFFSETS):
            dist = jnp.abs(j - t * QB - r)
            band_ok = (j < BAND) & (dist <= WINDOW)
            ndist = dist.astype(F32) * (-LOG2E)
            for head in range(ATT_Q_HEADS):
                slope = 2.0 ** (-8.0 * (head + 1) / ATT_Q_HEADS)
                sink = sink_ref[head] * LOG2E
                rest = jnp.where((j >= BAND) & (j < SINK_COL), 0.0,
                                 jnp.where(j == SINK_COL, sink, NEG_INF))
                bias_ref[t, head] = jnp.where(band_ok, slope * ndist, rest)

    low = lax.broadcasted_iota(jnp.int32, (QB, LANES), 1) < ATT_DH
    low_k = lax.broadcasted_iota(jnp.int32, (N_KEYS, LANES), 1) < ATT_DH

    def block_diag(x):
        zero = jnp.zeros_like(x)
        return jnp.concatenate([jnp.where(low_k, x, zero), jnp.where(low_k, zero, x)], axis=0)

    def body(it, carry):
        for u in range(HB_UNROLL):
            hb = it * HB_UNROLL + u
            row0 = n * ATT_ROWS + hb * QB
            start = pl.multiple_of(jnp.clip(row0 - WINDOW, 0, seq - BAND), QB)
            t = (row0 - start) // QB
            rows = pl.ds(pl.multiple_of(hb * QB, QB), QB)
            for jp in range(ATT_KV_HEADS // 2):
                ksl = slice(jp * LANES, (jp + 1) * LANES)
                k_bd = block_diag(jnp.concatenate(
                    [k_ref[pl.ds(start, BAND), ksl], km_ref[:, ksl]], axis=0))
                v_bd = block_diag(jnp.concatenate(
                    [v_ref[pl.ds(start, BAND), ksl], vm_ref[:, ksl]], axis=0))
                groups = [slice((jp * ATT_GROUP + g) * LANES, (jp * ATT_GROUP + g + 1) * LANES)
                          for g in range(ATT_GROUP)]
                q_stack = jnp.concatenate([q_ref[rows, grp] for grp in groups], axis=0)
                s = _dot_nt(q_stack, k_bd)
                probs, scales = [], []
                for g in range(ATT_GROUP):
                    p_halves, recip = [], []
                    for half in range(2):
                        head = (2 * jp + half) * ATT_GROUP + g
                        sg = (s[g * QB:(g + 1) * QB, half * N_KEYS:(half + 1) * N_KEYS]
                              + bias_ref[t, head])
                        p = jnp.exp2(sg - jnp.max(sg, axis=-1, keepdims=True))
                        recip.append(1.0 / jnp.sum(p, axis=-1, keepdims=True))
                        p_halves.append(p.astype(BF16))
                    probs.append(jnp.concatenate(p_halves, axis=1))
                    scales.append(jnp.where(low, recip[0], recip[1]))
                o = _dot(jnp.concatenate(probs, axis=0), v_bd)
                for g in range(ATT_GROUP):
                    o_ref[rows, groups[g]] = (o[g * QB:(g + 1) * QB] * scales[g]).astype(BF16)
        return carry

    lax.fori_loop(0, ATT_ROWS // QB // HB_UNROLL, body, 0)


def _attention_call(sink, proj, k_meta, v_meta, batch, seq):
    assert seq % ATT_ROWS == 0 and seq >= BAND
    kernel = functools.partial(_attention_kernel, seq=seq)
    kv_w = ATT_KV_HEADS * ATT_DH
    return pl.pallas_call(
        kernel,
        out_shape=jax.ShapeDtypeStruct((batch, seq, ATT_W), BF16),
        grid=(batch, seq // ATT_ROWS),
        in_specs=[
            pl.BlockSpec(memory_space=pltpu.SMEM),
            pl.BlockSpec((None, ATT_ROWS, ATT_W), lambda b, n: (b, n, OFF_AQ // ATT_W)),
            pl.BlockSpec((None, seq, kv_w), lambda b, n: (b, 0, OFF_AK // kv_w)),
            pl.BlockSpec((None, seq, kv_w), lambda b, n: (b, 0, OFF_AV // kv_w)),
            pl.BlockSpec((META_PAD, kv_w), lambda b, n: (0, 0)),
            pl.BlockSpec((META_PAD, kv_w), lambda b, n: (0, 0)),
        ],
        out_specs=pl.BlockSpec((None, ATT_ROWS, ATT_W), lambda b, n: (b, n, 0)),
        scratch_shapes=[pltpu.VMEM((N_OFFSETS, ATT_Q_HEADS, QB, N_KEYS), F32)],
        compiler_params=pltpu.CompilerParams(
            dimension_semantics=("arbitrary", "arbitrary"), vmem_limit_bytes=VMEM_LIMIT),
        name="attention",
    )(sink, proj, proj, proj, k_meta, v_meta)


FF_CHUNK = 256


def _mixffn_kernel(x_ref, ret_ref, att_ref, ga_ref, gb_ref, wr_ref, wa_ref, wo_ref,
                   nf_ref, wg_ref, wu_ref, wd_ref, nfin_ref, o_ref, h_ref, act_ref):
    ga = ga_ref[...].astype(F32)
    gb = gb_ref[...].astype(F32)
    merged = (_sigmoid(ga) * _dot(ret_ref[...], wr_ref[...])
              + _sigmoid(gb) * _dot(att_ref[...], wa_ref[...]))
    x = x_ref[...] + _dot(merged.astype(BF16), wo_ref[...])
    ms = jnp.mean(x * x, axis=-1, keepdims=True)
    h_ref[...] = (x * lax.rsqrt(ms + RMS_EPS) * nf_ref[...]).astype(BF16)
    for c in range(0, D_FF, FF_CHUNK):
        a = _dot(h_ref[...], wg_ref[:, c:c + FF_CHUNK])
        b = _dot(h_ref[...], wu_ref[:, c:c + FF_CHUNK])
        act_ref[:, c:c + FF_CHUNK] = (a * _sigmoid(a) * b).astype(BF16)
    y = x + _dot(act_ref[...], wd_ref[...])
    ms = jnp.mean(y * y, axis=-1, keepdims=True)
    o_ref[...] = y * lax.rsqrt(ms + RMS_EPS) * nfin_ref[...]


def _mixffn_call(x2d, ret, att, proj, wr, wa, wo, norm_ffn, wg, wu, wd, norm_final, tm):
    rows = x2d.shape[0]
    const = lambda shape: pl.BlockSpec(shape, lambda i: (0, 0), pipeline_mode=pl.Buffered(1))
    row_tile = lambda col: pl.BlockSpec((tm, D_MODEL), lambda i: (i, col))
    return pl.pallas_call(
        _mixffn_kernel,
        out_shape=jax.ShapeDtypeStruct((rows, D_MODEL), F32),
        grid=(rows // tm,),
        in_specs=[
            row_tile(0), row_tile(0), row_tile(0),
            row_tile(OFF_GA // D_MODEL), row_tile(OFF_GB // D_MODEL),
            const((RET_W, D_MODEL)), const((ATT_W, D_MODEL)), const((D_MODEL, D_MODEL)),
            const((1, D_MODEL)),
            const((D_MODEL, D_FF)), const((D_MODEL, D_FF)), const((D_FF, D_MODEL)),
            const((1, D_MODEL)),
        ],
        out_specs=row_tile(0),
        scratch_shapes=[pltpu.VMEM((tm, D_MODEL), BF16), pltpu.VMEM((tm, D_FF), BF16)],
        compiler_params=pltpu.CompilerParams(
            dimension_semantics=("parallel",), vmem_limit_bytes=VMEM_LIMIT),
        name="mixffn",
    )(x2d, ret, att, proj, proj, wr, wa, wo, norm_ffn, wg, wu, wd, norm_final)


def kernel(x, meta_tokens, w_in, ret_decay_logit_fwd, ret_decay_logit_bwd, ret_gn_gain, attn_sink,
           w_branch_ret, w_branch_att, w_out, norm_mix, norm_ffn, w_gate_up, w_down, norm_final):
    batch, seq, d = x.shape
    assert d == D_MODEL and seq % CHUNK == 0 and w_in.shape[0] == 1
    perm = jnp.asarray(ATT_PERM)

    w_in0 = w_in[0]
    w_in_b = jnp.concatenate(
        [w_in0[:, :SRC_AQ], w_in0[:, SRC_AQ:SRC_AK][:, perm], w_in0[:, SRC_GA:],
         w_in0[:, SRC_AK:SRC_GA]], axis=1).astype(BF16)
    wr = w_branch_ret[0].astype(BF16)
    wa = w_branch_att[0][perm, :].astype(BF16)
    wo = w_out[0].astype(BF16)
    wg = w_gate_up[0][:, :D_FF].astype(BF16)
    wu = w_gate_up[0][:, D_FF:].astype(BF16)
    wd = w_down[0].astype(BF16)
    nmix = norm_mix[0].reshape(1, D_MODEL)
    nffn = norm_ffn[0].reshape(1, D_MODEL)
    nfin = norm_final.reshape(1, D_MODEL)

    x2d = x.reshape(batch * seq, D_MODEL)
    proj = _inproj_call(x2d, nmix, w_in_b, 512)
    proj_meta = _inproj_call(meta_tokens.astype(x.dtype), nmix, w_in_b, N_META)
    proj3 = proj.reshape(batch, seq, IN_COLS)

    log_gammas = jnp.stack([jax.nn.log_sigmoid(ret_decay_logit_fwd[0].astype(F32)),
                            jax.nn.log_sigmoid(ret_decay_logit_bwd[0].astype(F32))])
    ret = _retention_call(log_gammas, proj3, proj_meta, ret_gn_gain[0].reshape(1, RET_W), batch, seq)

    kv_w = ATT_KV_HEADS * ATT_DH
    pad = ((0, META_PAD - N_META), (0, 0))
    k_meta = jnp.pad(proj_meta[:, OFF_AK:OFF_AK + kv_w], pad)
    v_meta = jnp.pad(proj_meta[:, OFF_AV:OFF_AV + kv_w], pad)
    att = _attention_call(attn_sink[0].astype(F32), proj3, k_meta, v_meta, batch, seq)

    out = _mixffn_call(x2d, ret.reshape(batch * seq, RET_W), att.reshape(batch * seq, ATT_W), proj,
                       wr, wa, wo, nffn, wg, wu, wd, nfin, 512)
    return out.reshape(batch, seq, D_MODEL)
```

```python
import functools

import numpy as np
import jax
import jax.numpy as jnp
from jax import lax
from jax.experimental import pallas as pl
from jax.experimental.pallas import tpu as pltpu

D_MODEL = 1024
N_META = 16
RET_HEADS = 8
RET_DK = 64
RET_DV = 128
CHUNK = 128
RET_W = RET_HEADS * RET_DV
ATT_Q_HEADS = 16
ATT_KV_HEADS = 4
ATT_GROUP = ATT_Q_HEADS // ATT_KV_HEADS
ATT_DH = 64
ATT_W = ATT_Q_HEADS * ATT_DH
WINDOW = 128
BLK = 128
D_FF = 2816
IN_SIZES = (512, 512, 1024, 1024, 1024, 256, 256, 1024, 1024)
IN_COLS = sum(IN_SIZES)
SRC_RQ, SRC_RK, SRC_RV, SRC_RG, SRC_AQ, SRC_AK, SRC_AV, SRC_GA, SRC_GB = (
    int(c) for c in np.cumsum((0,) + IN_SIZES[:-1]))
OFF_RQ, OFF_RK, OFF_RV, OFF_RG, OFF_AQ, OFF_GA, OFF_GB, OFF_AK, OFF_AV = (
    0, 512, 1024, 2048, 3072, 4096, 5120, 6144, 6400)
RMS_EPS = 1e-6
GN_EPS = 1e-5
NEG_INF = -1e30
LOG2E = 1.4426950408889634
ATT_Q_SCALE = ATT_DH ** -0.5 * LOG2E

LANES = 128
VMEM_LIMIT = 56 * 1024 * 1024

F32 = jnp.float32
BF16 = jnp.bfloat16


def _sigmoid(x):
    return 1.0 / (1.0 + jnp.exp(-x))


def _dot(a, b):
    return jnp.dot(a, b, preferred_element_type=F32)


def _dot_nt(a, b):
    return lax.dot_general(a, b, (((1,), (1,)), ((), ())), preferred_element_type=F32)


def _dot_tn(a, b):
    return lax.dot_general(a, b, (((0,), (0,)), ((), ())), preferred_element_type=F32)


IN_CHUNK = 512


def _inproj_kernel(x_ref, g_ref, w_ref, o_ref, h_ref):
    x = x_ref[...]
    ms = jnp.mean(x * x, axis=-1, keepdims=True)
    h_ref[...] = (x * lax.rsqrt(ms + RMS_EPS) * g_ref[...]).astype(BF16)
    for c in range(0, IN_COLS, IN_CHUNK):
        acc = _dot(h_ref[...], w_ref[:, c:c + IN_CHUNK])
        if OFF_AQ <= c < OFF_AQ + ATT_W:
            acc = acc * ATT_Q_SCALE
        o_ref[:, c:c + IN_CHUNK] = acc.astype(BF16)


def _inproj_call(x2d, gain, w_bf16, tm):
    rows = x2d.shape[0]
    return pl.pallas_call(
        _inproj_kernel,
        out_shape=jax.ShapeDtypeStruct((rows, IN_COLS), BF16),
        grid=(rows // tm,),
        in_specs=[
            pl.BlockSpec((tm, D_MODEL), lambda i: (i, 0)),
            pl.BlockSpec((1, D_MODEL), lambda i: (0, 0)),
            pl.BlockSpec((D_MODEL, IN_COLS), lambda i: (0, 0), pipeline_mode=pl.Buffered(1)),
        ],
        out_specs=pl.BlockSpec((tm, IN_COLS), lambda i: (i, 0)),
        scratch_shapes=[pltpu.VMEM((tm, D_MODEL), BF16)],
        compiler_params=pltpu.CompilerParams(
            dimension_semantics=("parallel",), vmem_limit_bytes=VMEM_LIMIT),
        name="inproj",
    )(x2d, gain, w_bf16)


RET_C = 256


def _retention_kernel(lg_ref, q_ref, k_ref, v_ref, km_ref, vm_ref, o_ref,
                      tq_ref, tk_ref, td_ref, sel_ref, ts_ref, tm_ref, u_ref, s_ref, i_ref,
                      *, n_chunks):
    pair = pl.program_id(0)
    C = RET_C
    k_scale = RET_DK ** -0.5

    @pl.when(pl.program_id(1) == 0)
    def _build_tables():
        lgf = [lg_ref[0, 2 * pair + hl] for hl in range(2)]
        lgb = [lg_ref[1, 2 * pair + hl] for hl in range(2)]
        row = lax.broadcasted_iota(jnp.int32, (C, LANES), 0).astype(F32)
        first = lax.broadcasted_iota(jnp.int32, (C, LANES), 1) < RET_DK
        pos_t = lax.broadcasted_iota(jnp.int32, (LANES, C), 1).astype(F32)
        first_t = lax.broadcasted_iota(jnp.int32, (LANES, C), 0) < RET_DK
        diff = (lax.broadcasted_iota(jnp.int32, (C, C), 0)
                - lax.broadcasted_iota(jnp.int32, (C, C), 1)).astype(F32)
        srow = lax.broadcasted_iota(jnp.int32, (LANES, 2 * RET_DV), 0)
        scol = lax.broadcasted_iota(jnp.int32, (LANES, 2 * RET_DV), 1)
        mrow = lax.broadcasted_iota(jnp.int32, (N_META, LANES), 0).astype(F32)
        mfirst = lax.broadcasted_iota(jnp.int32, (N_META, LANES), 1) < RET_DK

        def both(sel, fn, lg):
            return jnp.where(sel, fn(lg[0]), fn(lg[1]))

        tq_ref[0] = both(first, lambda l: jnp.exp((row + 1.0) * l), lgf)
        tq_ref[1] = both(first, lambda l: jnp.exp((C - row) * l), lgb)
        tk_ref[0] = both(first_t, lambda l: jnp.exp((C - 1.0 - pos_t) * l), lgf) * k_scale
        tk_ref[1] = both(first_t, lambda l: jnp.exp(pos_t * l), lgb) * k_scale
        tm_ref[...] = both(mfirst, lambda l: jnp.exp((N_META - 1.0 - mrow) * l), lgf) * k_scale
        for hl in range(2):
            td_ref[hl] = k_scale * jnp.where(diff >= 0, jnp.exp(jnp.maximum(diff, 0.0) * lgf[hl]),
                                             jnp.exp(jnp.maximum(-diff, 0.0) * lgb[hl]))
            sel_ref[hl] = jnp.where(first == (hl == 0), 1.0, 0.0).astype(BF16)
        own = (srow < RET_DK) == (scol < RET_DV)
        zeros = jnp.zeros((LANES, 2 * RET_DV), F32)
        ts_ref[0] = jnp.where(own, both(srow < RET_DK, lambda l: jnp.exp(zeros + C * l), lgf), 0.0)
        ts_ref[1] = jnp.where(own, both(srow < RET_DK, lambda l: jnp.exp(zeros + C * l), lgb), 0.0)
        ts_ref[2] = jnp.where(own, 1.0, 0.0)

    def head(x, hl):
        return x[:, hl * RET_DV:(hl + 1) * RET_DV]

    own_mask = ts_ref[2]

    for c in range(n_chunks):
        rows = slice(c * C, (c + 1) * C)
        kc = k_ref[rows, :]
        vc = v_ref[rows, :]
        k_t = kc.astype(F32).T
        lhs = jnp.concatenate([k_t * tk_ref[0], k_t * tk_ref[1]], axis=0).astype(BF16)
        u = _dot(lhs, vc)
        u_ref[c, 0:LANES, :] = u[0:LANES] * own_mask
        u_ref[c, LANES:2 * LANES, :] = u[LANES:2 * LANES] * own_mask
        q16 = q_ref[rows, :]
        for hl in range(2):
            s = _dot_nt(q16 * sel_ref[hl], kc) * td_ref[hl]
            i_ref[c, :, hl * RET_DV:(hl + 1) * RET_DV] = _dot(s.astype(BF16), head(vc, hl))

    sf = _dot_tn((km_ref[...].astype(F32) * tm_ref[...]).astype(BF16), vm_ref[...]) * own_mask
    for c in range(n_chunks):
        s_ref[c, 0:LANES, :] = sf.astype(BF16)
        if c + 1 < n_chunks:
            sf = sf * ts_ref[0] + u_ref[c, 0:LANES, :]
    sb = jnp.zeros((LANES, 2 * RET_DV), F32)
    for c in reversed(range(n_chunks)):
        s_ref[c, LANES:2 * LANES, :] = sb.astype(BF16)
        if c > 0:
            sb = sb * ts_ref[1] + u_ref[c, LANES:2 * LANES, :]

    for c in range(n_chunks):
        rows = slice(c * C, (c + 1) * C)
        qc = q_ref[rows, :].astype(F32)
        q_cross = jnp.concatenate([(qc * tq_ref[0]).astype(BF16),
                                   (qc * tq_ref[1]).astype(BF16)], axis=1)
        o_ref[rows, :] = (i_ref[c] + _dot(q_cross, s_ref[c])).astype(BF16)


def _retention_call(log_gammas, proj, proj_meta, batch, seq):
    assert seq % RET_C == 0
    n_chunks = seq // RET_C
    n_pairs = RET_HEADS // 2
    kernel = functools.partial(_retention_kernel, n_chunks=n_chunks)
    qb, kb = OFF_RQ // LANES, OFF_RK // LANES
    vb = OFF_RV // (2 * RET_DV)
    return pl.pallas_call(
        kernel,
        out_shape=jax.ShapeDtypeStruct((batch, seq, RET_W), BF16),
        grid=(n_pairs, batch),
        in_specs=[
            pl.BlockSpec(memory_space=pltpu.SMEM),
            pl.BlockSpec((None, seq, LANES), lambda p, b: (b, 0, qb + p)),
            pl.BlockSpec((None, seq, LANES), lambda p, b: (b, 0, kb + p)),
            pl.BlockSpec((None, seq, 2 * RET_DV), lambda p, b: (b, 0, vb + p)),
            pl.BlockSpec((N_META, LANES), lambda p, b: (0, kb + p)),
            pl.BlockSpec((N_META, 2 * RET_DV), lambda p, b: (0, vb + p)),
        ],
        out_specs=pl.BlockSpec((None, seq, 2 * RET_DV), lambda p, b: (b, 0, p)),
        scratch_shapes=[
            pltpu.VMEM((2, RET_C, LANES), F32),
            pltpu.VMEM((2, LANES, RET_C), F32),
            pltpu.VMEM((2, RET_C, RET_C), F32),
            pltpu.VMEM((2, RET_C, LANES), BF16),
            pltpu.VMEM((3, LANES, 2 * RET_DV), F32),
            pltpu.VMEM((N_META, LANES), F32),
            pltpu.VMEM((n_chunks, 2 * LANES, 2 * RET_DV), F32),
            pltpu.VMEM((n_chunks, 2 * LANES, 2 * RET_DV), BF16),
            pltpu.VMEM((n_chunks, RET_C, 2 * RET_DV), F32),
        ],
        compiler_params=pltpu.CompilerParams(
            dimension_semantics=("arbitrary", "arbitrary"), vmem_limit_bytes=VMEM_LIMIT),
        name="retention",
    )(log_gammas, proj, proj, proj, proj_meta, proj_meta)


def _att_head_perm():
    perm = []
    for j in range(ATT_KV_HEADS // 2):
        for g in range(ATT_GROUP):
            for half in range(2):
                head = (2 * j + half) * ATT_GROUP + g
                perm.extend(range(head * ATT_DH, (head + 1) * ATT_DH))
    return np.asarray(perm, dtype=np.int32)


ATT_PERM = _att_head_perm()
QB = 64
BAND = QB + 2 * WINDOW
META_PAD = 64
N_KEYS = BAND + META_PAD
ATT_ROWS = 256


N_OFFSETS = 2 * WINDOW // QB + 1
SINK_COL = BAND + N_META
HB_UNROLL = 4


def _attention_kernel(sink_ref, q_ref, k_ref, v_ref, km_ref, vm_ref, o_ref, bias_ref, *, seq):
    n = pl.program_id(1)

    @pl.when((pl.program_id(0) == 0) & (n == 0))
    def _build_bias():
        r = lax.broadcasted_iota(jnp.int32, (QB, N_KEYS), 0)
        j = lax.broadcasted_iota(jnp.int32, (QB, N_KEYS), 1)
        for t in range(N_OFFSETS):
            dist = jnp.abs(j - t * QB - r)
            band_ok = (j < BAND) & (dist <= WINDOW)
            ndist = dist.astype(F32) * (-LOG2E)
            for head in range(ATT_Q_HEADS):
                slope = 2.0 ** (-8.0 * (head + 1) / ATT_Q_HEADS)
                sink = sink_ref[head] * LOG2E
                rest = jnp.where((j >= BAND) & (j < SINK_COL), 0.0,
                                 jnp.where(j == SINK_COL, sink, NEG_INF))
                bias_ref[t, head] = jnp.where(band_ok, slope * ndist, rest)

    low_k = lax.broadcasted_iota(jnp.int32, (N_KEYS, LANES), 1) < ATT_DH

    def block_diag(x):
        zero = jnp.zeros_like(x)
        return jnp.concatenate([jnp.where(low_k, x, zero), jnp.where(low_k, zero, x)], axis=0)

    ones_bd = jnp.concatenate([jnp.where(low_k, 1.0, 0.0), jnp.where(low_k, 0.0, 1.0)],
                              axis=0).astype(BF16)

    def body(it, carry):
        for u in range(HB_UNROLL):
            hb = it * HB_UNROLL + u
            row0 = n * ATT_ROWS + hb * QB
            start = pl.multiple_of(jnp.clip(row0 - WINDOW, 0, seq - BAND), QB)
            t = (row0 - start) // QB
            rows = pl.ds(pl.multiple_of(hb * QB, QB), QB)
            for jp in range(ATT_KV_HEADS // 2):
                ksl = slice(jp * LANES, (jp + 1) * LANES)
                k_bd = block_diag(jnp.concatenate(
                    [k_ref[pl.ds(start, BAND), ksl], km_ref[:, ksl]], axis=0))
                v_bd = jnp.concatenate([block_diag(jnp.concatenate(
                    [v_ref[pl.ds(start, BAND), ksl], vm_ref[:, ksl]], axis=0)), ones_bd], axis=1)
                groups = [slice((jp * ATT_GROUP + g) * LANES, (jp * ATT_GROUP + g + 1) * LANES)
                          for g in range(ATT_GROUP)]
                q_stack = jnp.concatenate([q_ref[rows, grp] for grp in groups], axis=0)
                s = _dot_nt(q_stack, k_bd)
                probs = []
                for g in range(ATT_GROUP):
                    p_halves = []
                    for half in range(2):
                        head = (2 * jp + half) * ATT_GROUP + g
                        sg = (s[g * QB:(g + 1) * QB, half * N_KEYS:(half + 1) * N_KEYS]
                              + bias_ref[t, head])
                        p = jnp.exp2(sg - jnp.max(sg, axis=-1, keepdims=True))
                        p_halves.append(p.astype(BF16))
                    probs.append(jnp.concatenate(p_halves, axis=1))
                o = _dot(jnp.concatenate(probs, axis=0), v_bd)
                out = o[:, 0:LANES] / o[:, LANES:2 * LANES]
                for g in range(ATT_GROUP):
                    o_ref[rows, groups[g]] = out[g * QB:(g + 1) * QB].astype(BF16)
        return carry

    lax.fori_loop(0, ATT_ROWS // QB // HB_UNROLL, body, 0)


def _attention_call(sink, proj, k_meta, v_meta, batch, seq):
    assert seq % ATT_ROWS == 0 and seq >= BAND
    kernel = functools.partial(_attention_kernel, seq=seq)
    kv_w = ATT_KV_HEADS * ATT_DH
    return pl.pallas_call(
        kernel,
        out_shape=jax.ShapeDtypeStruct((batch, seq, ATT_W), BF16),
        grid=(batch, seq // ATT_ROWS),
        in_specs=[
            pl.BlockSpec(memory_space=pltpu.SMEM),
            pl.BlockSpec((None, ATT_ROWS, ATT_W), lambda b, n: (b, n, OFF_AQ // ATT_W)),
            pl.BlockSpec((None, seq, kv_w), lambda b, n: (b, 0, OFF_AK // kv_w)),
            pl.BlockSpec((None, seq, kv_w), lambda b, n: (b, 0, OFF_AV // kv_w)),
            pl.BlockSpec((META_PAD, kv_w), lambda b, n: (0, 0)),
            pl.BlockSpec((META_PAD, kv_w), lambda b, n: (0, 0)),
        ],
        out_specs=pl.BlockSpec((None, ATT_ROWS, ATT_W), lambda b, n: (b, n, 0)),
        scratch_shapes=[pltpu.VMEM((N_OFFSETS, ATT_Q_HEADS, QB, N_KEYS), F32)],
        compiler_params=pltpu.CompilerParams(
            dimension_semantics=("arbitrary", "arbitrary"), vmem_limit_bytes=VMEM_LIMIT),
        name="attention",
    )(sink, proj, proj, proj, k_meta, v_meta)


FF_CHUNK = 256


def _mixffn_kernel(x_ref, ret_ref, rg_ref, gain_ref, att_ref, ga_ref, gb_ref, wr_ref, wa_ref, wo_ref,
                   nf_ref, wg_ref, wu_ref, wd_ref, nfin_ref, o_ref, r_ref, h_ref, act_ref):
    gb = gb_ref[...].astype(F32)
    att_part = _sigmoid(gb) * _dot(att_ref[...], wa_ref[...])
    for h in range(RET_HEADS):
        sl = slice(h * RET_DV, (h + 1) * RET_DV)
        o = ret_ref[:, sl].astype(F32)
        d = o - jnp.mean(o, axis=-1, keepdims=True)
        var = jnp.mean(d * d, axis=-1, keepdims=True)
        g = rg_ref[:, sl].astype(F32)
        r_ref[:, sl] = (g * _sigmoid(g) * (d * lax.rsqrt(var + GN_EPS) * gain_ref[:, sl])).astype(BF16)
    ga = ga_ref[...].astype(F32)
    merged = _sigmoid(ga) * _dot(r_ref[...], wr_ref[...]) + att_part
    x = x_ref[...] + _dot(merged.astype(BF16), wo_ref[...])
    ms = jnp.mean(x * x, axis=-1, keepdims=True)
    h_ref[...] = (x * lax.rsqrt(ms + RMS_EPS) * nf_ref[...]).astype(BF16)
    for c in range(0, D_FF, FF_CHUNK):
        a = _dot(h_ref[...], wg_ref[:, c:c + FF_CHUNK])
        b = _dot(h_ref[...], wu_ref[:, c:c + FF_CHUNK])
        act_ref[:, c:c + FF_CHUNK] = (a * _sigmoid(a) * b).astype(BF16)
    y = x + _dot(act_ref[...], wd_ref[...])
    ms = jnp.mean(y * y, axis=-1, keepdims=True)
    o_ref[...] = y * lax.rsqrt(ms + RMS_EPS) * nfin_ref[...]


def _mixffn_call(x2d, ret, gn_gain, att, proj, wr, wa, wo, norm_ffn, wg, wu, wd, norm_final, tm):
    rows = x2d.shape[0]
    const = lambda shape: pl.BlockSpec(shape, lambda i: (0, 0), pipeline_mode=pl.Buffered(1))
    row_tile = lambda col: pl.BlockSpec((tm, D_MODEL), lambda i: (i, col))
    return pl.pallas_call(
        _mixffn_kernel,
        out_shape=jax.ShapeDtypeStruct((rows, D_MODEL), F32),
        grid=(rows // tm,),
        in_specs=[
            row_tile(0), row_tile(0), row_tile(OFF_RG // RET_W), const((1, RET_W)), row_tile(0),
            row_tile(OFF_GA // D_MODEL), row_tile(OFF_GB // D_MODEL),
            const((RET_W, D_MODEL)), const((ATT_W, D_MODEL)), const((D_MODEL, D_MODEL)),
            const((1, D_MODEL)),
            const((D_MODEL, D_FF)), const((D_MODEL, D_FF)), const((D_FF, D_MODEL)),
            const((1, D_MODEL)),
        ],
        out_specs=row_tile(0),
        scratch_shapes=[pltpu.VMEM((tm, RET_W), BF16), pltpu.VMEM((tm, D_MODEL), BF16),
                        pltpu.VMEM((tm, D_FF), BF16)],
        compiler_params=pltpu.CompilerParams(
            dimension_semantics=("parallel",), vmem_limit_bytes=VMEM_LIMIT),
        name="mixffn",
    )(x2d, ret, proj, gn_gain, att, proj, proj, wr, wa, wo, norm_ffn, wg, wu, wd, norm_final)


def kernel(x, meta_tokens, w_in, ret_decay_logit_fwd, ret_decay_logit_bwd, ret_gn_gain, attn_sink,
           w_branch_ret, w_branch_att, w_out, norm_mix, norm_ffn, w_gate_up, w_down, norm_final):
    batch, seq, d = x.shape
    assert d == D_MODEL and seq % CHUNK == 0 and w_in.shape[0] == 1
    perm = jnp.asarray(ATT_PERM)

    w_in0 = w_in[0]
    w_in_b = jnp.concatenate(
        [w_in0[:, :SRC_AQ], w_in0[:, SRC_AQ:SRC_AK][:, perm], w_in0[:, SRC_GA:],
         w_in0[:, SRC_AK:SRC_GA]], axis=1).astype(BF16)
    wr = w_branch_ret[0].astype(BF16)
    wa = w_branch_att[0][perm, :].astype(BF16)
    wo = w_out[0].astype(BF16)
    wg = w_gate_up[0][:, :D_FF].astype(BF16)
    wu = w_gate_up[0][:, D_FF:].astype(BF16)
    wd = w_down[0].astype(BF16)
    nmix = norm_mix[0].reshape(1, D_MODEL)
    nffn = norm_ffn[0].reshape(1, D_MODEL)
    nfin = norm_final.reshape(1, D_MODEL)

    x2d = x.reshape(batch * seq, D_MODEL)
    proj = _inproj_call(x2d, nmix, w_in_b, 512)
    proj_meta = _inproj_call(meta_tokens.astype(x.dtype), nmix, w_in_b, N_META)
    proj3 = proj.reshape(batch, seq, IN_COLS)

    log_gammas = jnp.stack([jax.nn.log_sigmoid(ret_decay_logit_fwd[0].astype(F32)),
                            jax.nn.log_sigmoid(ret_decay_logit_bwd[0].astype(F32))])
    ret = _retention_call(log_gammas, proj3, proj_meta, batch, seq)

    kv_w = ATT_KV_HEADS * ATT_DH
    pad = ((0, META_PAD - N_META), (0, 0))
    k_meta = jnp.pad(proj_meta[:, OFF_AK:OFF_AK + kv_w], pad)
    v_meta = jnp.pad(proj_meta[:, OFF_AV:OFF_AV + kv_w], pad)
    att = _attention_call(attn_sink[0].astype(F32), proj3, k_meta, v_meta, batch, seq)

    out = _mixffn_call(x2d, ret.reshape(batch * seq, RET_W), ret_gn_gain[0].reshape(1, RET_W),
                       att.reshape(batch * seq, ATT_W), proj,
                       wr, wa, wo, nffn, wg, wu, wd, nfin, 512)
    return out.reshape(batch, seq, D_MODEL)
```

```python
import functools

import numpy as np
import jax
import jax.numpy as jnp
from jax import lax
from jax.experimental import pallas as pl
from jax.experimental.pallas import tpu as pltpu

D_MODEL = 1024
N_META = 16
RET_HEADS = 8
RET_DK = 64
RET_DV = 128
CHUNK = 128
RET_W = RET_HEADS * RET_DV
ATT_Q_HEADS = 16
ATT_KV_HEADS = 4
ATT_GROUP = ATT_Q_HEADS // ATT_KV_HEADS
ATT_DH = 64
ATT_W = ATT_Q_HEADS * ATT_DH
WINDOW = 128
BLK = 128
D_FF = 2816
IN_SIZES = (512, 512, 1024, 1024, 1024, 256, 256, 1024, 1024)
IN_COLS = sum(IN_SIZES)
SRC_RQ, SRC_RK, SRC_RV, SRC_RG, SRC_AQ, SRC_AK, SRC_AV, SRC_GA, SRC_GB = (
    int(c) for c in np.cumsum((0,) + IN_SIZES[:-1]))
OFF_RQ, OFF_RK, OFF_RV, OFF_RG, OFF_AQ, OFF_GA, OFF_GB, OFF_AK, OFF_AV = (
    0, 512, 1024, 2048, 3072, 4096, 5120, 6144, 6400)
RMS_EPS = 1e-6
GN_EPS = 1e-5
NEG_INF = -1e30
LOG2E = 1.4426950408889634
ATT_Q_SCALE = ATT_DH ** -0.5 * LOG2E

LANES = 128
VMEM_LIMIT = 56 * 1024 * 1024

F32 = jnp.float32
BF16 = jnp.bfloat16


def _sigmoid(x):
    return 1.0 / (1.0 + jnp.exp(-x))


def _dot(a, b):
    return jnp.dot(a, b, preferred_element_type=F32)


def _dot_nt(a, b):
    return lax.dot_general(a, b, (((1,), (1,)), ((), ())), preferred_element_type=F32)


def _dot_tn(a, b):
    return lax.dot_general(a, b, (((0,), (0,)), ((), ())), preferred_element_type=F32)


IN_CHUNK = 512


def _inproj_kernel(x_ref, g_ref, w_ref, o_ref, h_ref):
    x = x_ref[...]
    ms = jnp.mean(x * x, axis=-1, keepdims=True)
    h_ref[...] = (x * lax.rsqrt(ms + RMS_EPS) * g_ref[...]).astype(BF16)
    for c in range(0, IN_COLS, IN_CHUNK):
        acc = _dot(h_ref[...], w_ref[:, c:c + IN_CHUNK])
        if OFF_AQ <= c < OFF_AQ + ATT_W:
            acc = acc * ATT_Q_SCALE
        o_ref[:, c:c + IN_CHUNK] = acc.astype(BF16)


def _inproj_call(x2d, gain, w_bf16, tm):
    rows = x2d.shape[0]
    return pl.pallas_call(
        _inproj_kernel,
        out_shape=jax.ShapeDtypeStruct((rows, IN_COLS), BF16),
        grid=(rows // tm,),
        in_specs=[
            pl.BlockSpec((tm, D_MODEL), lambda i: (i, 0)),
            pl.BlockSpec((1, D_MODEL), lambda i: (0, 0)),
            pl.BlockSpec((D_MODEL, IN_COLS), lambda i: (0, 0), pipeline_mode=pl.Buffered(1)),
        ],
        out_specs=pl.BlockSpec((tm, IN_COLS), lambda i: (i, 0)),
        scratch_shapes=[pltpu.VMEM((tm, D_MODEL), BF16)],
        compiler_params=pltpu.CompilerParams(
            dimension_semantics=("parallel",), vmem_limit_bytes=VMEM_LIMIT),
        name="inproj",
    )(x2d, gain, w_bf16)


RET_C = 256


def _retention_kernel(lg_ref, q_ref, k_ref, v_ref, km_ref, vm_ref, o_ref,
                      tq_ref, tk_ref, td_ref, sel_ref, ts_ref, tm_ref, u_ref, s_ref, i_ref,
                      *, n_chunks):
    pair = pl.program_id(0)
    C = RET_C
    k_scale = RET_DK ** -0.5

    @pl.when(pl.program_id(1) == 0)
    def _build_tables():
        lgf = [lg_ref[0, 2 * pair + hl] for hl in range(2)]
        lgb = [lg_ref[1, 2 * pair + hl] for hl in range(2)]
        row = lax.broadcasted_iota(jnp.int32, (C, LANES), 0).astype(F32)
        first = lax.broadcasted_iota(jnp.int32, (C, LANES), 1) < RET_DK
        pos_t = lax.broadcasted_iota(jnp.int32, (LANES, C), 1).astype(F32)
        first_t = lax.broadcasted_iota(jnp.int32, (LANES, C), 0) < RET_DK
        diff = (lax.broadcasted_iota(jnp.int32, (C, C), 0)
                - lax.broadcasted_iota(jnp.int32, (C, C), 1)).astype(F32)
        srow = lax.broadcasted_iota(jnp.int32, (LANES, 2 * RET_DV), 0)
        scol = lax.broadcasted_iota(jnp.int32, (LANES, 2 * RET_DV), 1)
        mrow = lax.broadcasted_iota(jnp.int32, (N_META, LANES), 0).astype(F32)
        mfirst = lax.broadcasted_iota(jnp.int32, (N_META, LANES), 1) < RET_DK

        def both(sel, fn, lg):
            return jnp.where(sel, fn(lg[0]), fn(lg[1]))

        tq_ref[0] = both(first, lambda l: jnp.exp((row + 1.0) * l), lgf)
        tq_ref[1] = both(first, lambda l: jnp.exp((C - row) * l), lgb)
        tk_ref[0] = both(first_t, lambda l: jnp.exp((C - 1.0 - pos_t) * l), lgf) * k_scale
        tk_ref[1] = both(first_t, lambda l: jnp.exp(pos_t * l), lgb) * k_scale
        tm_ref[...] = both(mfirst, lambda l: jnp.exp((N_META - 1.0 - mrow) * l), lgf) * k_scale
        for hl in range(2):
            td_ref[hl] = k_scale * jnp.where(diff >= 0, jnp.exp(jnp.maximum(diff, 0.0) * lgf[hl]),
                                             jnp.exp(jnp.maximum(-diff, 0.0) * lgb[hl]))
            sel_ref[hl] = jnp.where(first == (hl == 0), 1.0, 0.0).astype(BF16)
        own = (srow < RET_DK) == (scol < RET_DV)
        zeros = jnp.zeros((LANES, 2 * RET_DV), F32)
        ts_ref[0] = jnp.where(own, both(srow < RET_DK, lambda l: jnp.exp(zeros + C * l), lgf), 0.0)
        ts_ref[1] = jnp.where(own, both(srow < RET_DK, lambda l: jnp.exp(zeros + C * l), lgb), 0.0)
        ts_ref[2] = jnp.where(own, 1.0, 0.0)

    def head(x, hl):
        return x[:, hl * RET_DV:(hl + 1) * RET_DV]

    own_mask = ts_ref[2]

    for c in range(n_chunks):
        rows = slice(c * C, (c + 1) * C)
        kc = k_ref[rows, :]
        vc = v_ref[rows, :]
        k_t = kc.astype(F32).T
        lhs = jnp.concatenate([k_t * tk_ref[0], k_t * tk_ref[1]], axis=0).astype(BF16)
        u = _dot(lhs, vc)
        u_ref[c, 0:LANES, :] = u[0:LANES] * own_mask
        u_ref[c, LANES:2 * LANES, :] = u[LANES:2 * LANES] * own_mask
        q16 = q_ref[rows, :]
        for hl in range(2):
            s = _dot_nt(q16 * sel_ref[hl], kc) * td_ref[hl]
            i_ref[c, :, hl * RET_DV:(hl + 1) * RET_DV] = _dot(s.astype(BF16), head(vc, hl))

    sf = _dot_tn((km_ref[...].astype(F32) * tm_ref[...]).astype(BF16), vm_ref[...]) * own_mask
    for c in range(n_chunks):
        s_ref[c, 0:LANES, :] = sf.astype(BF16)
        if c + 1 < n_chunks:
            sf = sf * ts_ref[0] + u_ref[c, 0:LANES, :]
    sb = jnp.zeros((LANES, 2 * RET_DV), F32)
    for c in reversed(range(n_chunks)):
        s_ref[c, LANES:2 * LANES, :] = sb.astype(BF16)
        if c > 0:
            sb = sb * ts_ref[1] + u_ref[c, LANES:2 * LANES, :]

    for c in range(n_chunks):
        rows = slice(c * C, (c + 1) * C)
        qc = q_ref[rows, :].astype(F32)
        q_cross = jnp.concatenate([(qc * tq_ref[0]).astype(BF16),
                                   (qc * tq_ref[1]).astype(BF16)], axis=1)
        o_ref[rows, :] = (i_ref[c] + _dot(q_cross, s_ref[c])).astype(BF16)


def _retention_call(log_gammas, proj, proj_meta, batch, seq):
    assert seq % RET_C == 0
    n_chunks = seq // RET_C
    n_pairs = RET_HEADS // 2
    kernel = functools.partial(_retention_kernel, n_chunks=n_chunks)
    qb, kb = OFF_RQ // LANES, OFF_RK // LANES
    vb = OFF_RV // (2 * RET_DV)
    return pl.pallas_call(
        kernel,
        out_shape=jax.ShapeDtypeStruct((batch, seq, RET_W), BF16),
        grid=(n_pairs, batch),
        in_specs=[
            pl.BlockSpec(memory_space=pltpu.SMEM),
            pl.BlockSpec((None, seq, LANES), lambda p, b: (b, 0, qb + p)),
            pl.BlockSpec((None, seq, LANES), lambda p, b: (b, 0, kb + p)),
            pl.BlockSpec((None, seq, 2 * RET_DV), lambda p, b: (b, 0, vb + p)),
            pl.BlockSpec((N_META, LANES), lambda p, b: (0, kb + p)),
            pl.BlockSpec((N_META, 2 * RET_DV), lambda p, b: (0, vb + p)),
        ],
        out_specs=pl.BlockSpec((None, seq, 2 * RET_DV), lambda p, b: (b, 0, p)),
        scratch_shapes=[
            pltpu.VMEM((2, RET_C, LANES), F32),
            pltpu.VMEM((2, LANES, RET_C), F32),
            pltpu.VMEM((2, RET_C, RET_C), F32),
            pltpu.VMEM((2, RET_C, LANES), BF16),
            pltpu.VMEM((3, LANES, 2 * RET_DV), F32),
            pltpu.VMEM((N_META, LANES), F32),
            pltpu.VMEM((n_chunks, 2 * LANES, 2 * RET_DV), F32),
            pltpu.VMEM((n_chunks, 2 * LANES, 2 * RET_DV), BF16),
            pltpu.VMEM((n_chunks, RET_C, 2 * RET_DV), F32),
        ],
        compiler_params=pltpu.CompilerParams(
            dimension_semantics=("arbitrary", "arbitrary"), vmem_limit_bytes=VMEM_LIMIT),
        name="retention",
    )(log_gammas, proj, proj, proj, proj_meta, proj_meta)


def _pair_heads(w):
    w5 = w.reshape(ATT_KV_HEADS // 2, 2, ATT_GROUP, ATT_DH, w.shape[-1])
    return w5.transpose(0, 2, 1, 3, 4).reshape(w.shape)


QB = 64
BAND = QB + 2 * WINDOW
META_PAD = 64
N_KEYS = BAND + META_PAD
ATT_ROWS = 1024
N_OFFSETS = 2 * WINDOW // QB + 1
SINK_COL = BAND + N_META
HB_UNROLL = 8


def _attention_kernel(sink_ref, q_ref, k_ref, v_ref, km_ref, vm_ref, o_ref, bias_ref, *, seq):
    n = pl.program_id(1)

    @pl.when((pl.program_id(0) == 0) & (n == 0))
    def _build_bias():
        r = lax.broadcasted_iota(jnp.int32, (QB, N_KEYS), 0)
        j = lax.broadcasted_iota(jnp.int32, (QB, N_KEYS), 1)
        for t in range(N_OFFSETS):
            dist = jnp.abs(j - t * QB - r)
            band_ok = (j < BAND) & (dist <= WINDOW)
            ndist = dist.astype(F32) * (-LOG2E)
            for head in range(ATT_Q_HEADS):
                slope = 2.0 ** (-8.0 * (head + 1) / ATT_Q_HEADS)
                sink = sink_ref[head] * LOG2E
                rest = jnp.where((j >= BAND) & (j < SINK_COL), 0.0,
                                 jnp.where(j == SINK_COL, sink, NEG_INF))
                bias_ref[t, head] = jnp.where(band_ok, slope * ndist, rest)

    low_k = lax.broadcasted_iota(jnp.int32, (N_KEYS, LANES), 1) < ATT_DH

    def block_diag(x):
        zero = jnp.zeros_like(x)
        return jnp.concatenate([jnp.where(low_k, x, zero), jnp.where(low_k, zero, x)], axis=0)

    ones_bd = jnp.concatenate([jnp.where(low_k, 1.0, 0.0), jnp.where(low_k, 0.0, 1.0)],
                              axis=0).astype(BF16)

    def body(it, carry):
        for u in range(HB_UNROLL):
            hb = it * HB_UNROLL + u
            row0 = n * ATT_ROWS + hb * QB
            start = pl.multiple_of(jnp.clip(row0 - WINDOW, 0, seq - BAND), QB)
            t = (row0 - start) // QB
            rows = pl.ds(pl.multiple_of(hb * QB, QB), QB)
            for jp in range(ATT_KV_HEADS // 2):
                ksl = slice(jp * LANES, (jp + 1) * LANES)
                k_bd = block_diag(jnp.concatenate(
                    [k_ref[pl.ds(start, BAND), ksl], km_ref[:, ksl]], axis=0))
                v_bd = jnp.concatenate([block_diag(jnp.concatenate(
                    [v_ref[pl.ds(start, BAND), ksl], vm_ref[:, ksl]], axis=0)), ones_bd], axis=1)
                groups = [slice((jp * ATT_GROUP + g) * LANES, (jp * ATT_GROUP + g + 1) * LANES)
                          for g in range(ATT_GROUP)]
                q_stack = jnp.concatenate([q_ref[rows, grp] for grp in groups], axis=0)
                s = _dot_nt(q_stack, k_bd)
                probs = []
                for g in range(ATT_GROUP):
                    p_halves = []
                    for half in range(2):
                        head = (2 * jp + half) * ATT_GROUP + g
                        sg = (s[g * QB:(g + 1) * QB, half * N_KEYS:(half + 1) * N_KEYS]
                              + bias_ref[t, head])
                        p = jnp.exp2(sg - jnp.max(sg, axis=-1, keepdims=True))
                        p_halves.append(p.astype(BF16))
                    probs.append(jnp.concatenate(p_halves, axis=1))
                o = _dot(jnp.concatenate(probs, axis=0), v_bd)
                out = o[:, 0:LANES] / o[:, LANES:2 * LANES]
                for g in range(ATT_GROUP):
                    o_ref[rows, groups[g]] = out[g * QB:(g + 1) * QB].astype(BF16)
        return carry

    lax.fori_loop(0, ATT_ROWS // QB // HB_UNROLL, body, 0)


def _attention_call(sink, proj, k_meta, v_meta, batch, seq):
    assert seq % ATT_ROWS == 0 and seq >= BAND
    kernel = functools.partial(_attention_kernel, seq=seq)
    kv_w = ATT_KV_HEADS * ATT_DH
    return pl.pallas_call(
        kernel,
        out_shape=jax.ShapeDtypeStruct((batch, seq, ATT_W), BF16),
        grid=(batch, seq // ATT_ROWS),
        in_specs=[
            pl.BlockSpec(memory_space=pltpu.SMEM),
            pl.BlockSpec((None, ATT_ROWS, ATT_W), lambda b, n: (b, n, OFF_AQ // ATT_W)),
            pl.BlockSpec((None, seq, kv_w), lambda b, n: (b, 0, OFF_AK // kv_w)),
            pl.BlockSpec((None, seq, kv_w), lambda b, n: (b, 0, OFF_AV // kv_w)),
            pl.BlockSpec((META_PAD, kv_w), lambda b, n: (0, 0)),
            pl.BlockSpec((META_PAD, kv_w), lambda b, n: (0, 0)),
        ],
        out_specs=pl.BlockSpec((None, ATT_ROWS, ATT_W), lambda b, n: (b, n, 0)),
        scratch_shapes=[pltpu.VMEM((N_OFFSETS, ATT_Q_HEADS, QB, N_KEYS), F32)],
        compiler_params=pltpu.CompilerParams(
            dimension_semantics=("arbitrary", "arbitrary"), vmem_limit_bytes=VMEM_LIMIT),
        name="attention",
    )(sink, proj, proj, proj, k_meta, v_meta)


FF_CHUNK = 256


def _mixffn_kernel(x_ref, ret_ref, rg_ref, gain_ref, att_ref, ga_ref, gb_ref, wr_ref, wa_ref, wo_ref,
                   nf_ref, wg_ref, wu_ref, wd_ref, nfin_ref, o_ref, r_ref, h_ref, act_ref):
    gb = gb_ref[...].astype(F32)
    att_part = _sigmoid(gb) * _dot(att_ref[...], wa_ref[...])
    for h in range(RET_HEADS):
        sl = slice(h * RET_DV, (h + 1) * RET_DV)
        o = ret_ref[:, sl].astype(F32)
        d = o - jnp.mean(o, axis=-1, keepdims=True)
        var = jnp.mean(d * d, axis=-1, keepdims=True)
        g = rg_ref[:, sl].astype(F32)
        r_ref[:, sl] = (g * _sigmoid(g) * (d * lax.rsqrt(var + GN_EPS) * gain_ref[:, sl])).astype(BF16)
    ga = ga_ref[...].astype(F32)
    merged = _sigmoid(ga) * _dot(r_ref[...], wr_ref[...]) + att_part
    x = x_ref[...] + _dot(merged.astype(BF16), wo_ref[...])
    ms = jnp.mean(x * x, axis=-1, keepdims=True)
    h_ref[...] = (x * lax.rsqrt(ms + RMS_EPS) * nf_ref[...]).astype(BF16)
    for c in range(0, D_FF, FF_CHUNK):
        a = _dot(h_ref[...], wg_ref[:, c:c + FF_CHUNK])
        b = _dot(h_ref[...], wu_ref[:, c:c + FF_CHUNK])
        act_ref[:, c:c + FF_CHUNK] = (a * _sigmoid(a) * b).astype(BF16)
    y = x + _dot(act_ref[...], wd_ref[...])
    ms = jnp.mean(y * y, axis=-1, keepdims=True)
    o_ref[...] = y * lax.rsqrt(ms + RMS_EPS) * nfin_ref[...]


def _mixffn_call(x2d, ret, gn_gain, att, proj, wr, wa, wo, norm_ffn, wg, wu, wd, norm_final, tm):
    rows = x2d.shape[0]
    const = lambda shape: pl.BlockSpec(shape, lambda i: (0, 0), pipeline_mode=pl.Buffered(1))
    row_tile = lambda col: pl.BlockSpec((tm, D_MODEL), lambda i: (i, col))
    return pl.pallas_call(
        _mixffn_kernel,
        out_shape=jax.ShapeDtypeStruct((rows, D_MODEL), F32),
        grid=(rows // tm,),
        in_specs=[
            row_tile(0), row_tile(0), row_tile(OFF_RG // RET_W), const((1, RET_W)), row_tile(0),
            row_tile(OFF_GA // D_MODEL), row_tile(OFF_GB // D_MODEL),
            const((RET_W, D_MODEL)), const((ATT_W, D_MODEL)), const((D_MODEL, D_MODEL)),
            const((1, D_MODEL)),
            const((D_MODEL, D_FF)), const((D_MODEL, D_FF)), const((D_FF, D_MODEL)),
            const((1, D_MODEL)),
        ],
        out_specs=row_tile(0),
        scratch_shapes=[pltpu.VMEM((tm, RET_W), BF16), pltpu.VMEM((tm, D_MODEL), BF16),
                        pltpu.VMEM((tm, D_FF), BF16)],
        compiler_params=pltpu.CompilerParams(
            dimension_semantics=("parallel",), vmem_limit_bytes=VMEM_LIMIT),
        name="mixffn",
    )(x2d, ret, proj, gn_gain, att, proj, proj, wr, wa, wo, norm_ffn, wg, wu, wd, norm_final)


def kernel(x, meta_tokens, w_in, ret_decay_logit_fwd, ret_decay_logit_bwd, ret_gn_gain, attn_sink,
           w_branch_ret, w_branch_att, w_out, norm_mix, norm_ffn, w_gate_up, w_down, norm_final):
    batch, seq, d = x.shape
    assert d == D_MODEL and seq % CHUNK == 0 and w_in.shape[0] == 1

    w_in0 = w_in[0]
    w_in_b = jnp.concatenate(
        [w_in0[:, :SRC_AQ], _pair_heads(w_in0[:, SRC_AQ:SRC_AK].T).T, w_in0[:, SRC_GA:],
         w_in0[:, SRC_AK:SRC_GA]], axis=1).astype(BF16)
    wr = w_branch_ret[0].astype(BF16)
    wa = _pair_heads(w_branch_att[0]).astype(BF16)
    wo = w_out[0].astype(BF16)
    wg = w_gate_up[0][:, :D_FF].astype(BF16)
    wu = w_gate_up[0][:, D_FF:].astype(BF16)
    wd = w_down[0].astype(BF16)
    nmix = norm_mix[0].reshape(1, D_MODEL)
    nffn = norm_ffn[0].reshape(1, D_MODEL)
    nfin = norm_final.reshape(1, D_MODEL)

    x2d = x.reshape(batch * seq, D_MODEL)
    proj = _inproj_call(x2d, nmix, w_in_b, 1024)
    proj_meta = _inproj_call(meta_tokens.astype(x.dtype), nmix, w_in_b, N_META)
    proj3 = proj.reshape(batch, seq, IN_COLS)

    log_gammas = jnp.stack([jax.nn.log_sigmoid(ret_decay_logit_fwd[0].astype(F32)),
                            jax.nn.log_sigmoid(ret_decay_logit_bwd[0].astype(F32))])
    ret = _retention_call(log_gammas, proj3, proj_meta, batch, seq)

    kv_w = ATT_KV_HEADS * ATT_DH
    pad = ((0, META_PAD - N_META), (0, 0))
    k_meta = jnp.pad(proj_meta[:, OFF_AK:OFF_AK + kv_w], pad)
    v_meta = jnp.pad(proj_meta[:, OFF_AV:OFF_AV + kv_w], pad)
    att = _attention_call(attn_sink[0].astype(F32), proj3, k_meta, v_meta, batch, seq)

    out = _mixffn_call(x2d, ret.reshape(batch * seq, RET_W), ret_gn_gain[0].reshape(1, RET_W),
                       att.reshape(batch * seq, ATT_W), proj,
                       wr, wa, wo, nffn, wg, wu, wd, nfin, 512)
    return out.reshape(batch, seq, D_MODEL)
```

```python
import functools

import numpy as np
import jax
import jax.numpy as jnp
from jax import lax
from jax.experimental import pallas as pl
from jax.experimental.pallas import tpu as pltpu

D_MODEL = 1024
N_META = 16
RET_HEADS = 8
RET_DK = 64
RET_DV = 128
CHUNK = 128
RET_W = RET_HEADS * RET_DV
ATT_Q_HEADS = 16
ATT_KV_HEADS = 4
ATT_GROUP = ATT_Q_HEADS // ATT_KV_HEADS
ATT_DH = 64
ATT_W = ATT_Q_HEADS * ATT_DH
WINDOW = 128
BLK = 128
D_FF = 2816
IN_SIZES = (512, 512, 1024, 1024, 1024, 256, 256, 1024, 1024)
IN_COLS = sum(IN_SIZES)
SRC_RQ, SRC_RK, SRC_RV, SRC_RG, SRC_AQ, SRC_AK, SRC_AV, SRC_GA, SRC_GB = (
    int(c) for c in np.cumsum((0,) + IN_SIZES[:-1]))
OFF_RQ, OFF_RK, OFF_RV, OFF_RG, OFF_AQ, OFF_GA, OFF_GB, OFF_AK, OFF_AV = (
    0, 512, 1024, 2048, 3072, 4096, 5120, 6144, 6400)
RMS_EPS = 1e-6
GN_EPS = 1e-5
NEG_INF = -1e30
LOG2E = 1.4426950408889634
ATT_Q_SCALE = ATT_DH ** -0.5 * LOG2E

LANES = 128
VMEM_LIMIT = 56 * 1024 * 1024

F32 = jnp.float32
BF16 = jnp.bfloat16


def _sigmoid(x):
    return 1.0 / (1.0 + jnp.exp(-x))


def _dot(a, b):
    return jnp.dot(a, b, preferred_element_type=F32)


def _dot_nt(a, b):
    return lax.dot_general(a, b, (((1,), (1,)), ((), ())), preferred_element_type=F32)


def _dot_tn(a, b):
    return lax.dot_general(a, b, (((0,), (0,)), ((), ())), preferred_element_type=F32)


IN_CHUNK = 512


def _inproj_kernel(x_ref, g_ref, w_ref, o_ref, h_ref):
    x = x_ref[...]
    ms = jnp.mean(x * x, axis=-1, keepdims=True)
    h_ref[...] = (x * lax.rsqrt(ms + RMS_EPS) * g_ref[...]).astype(BF16)
    for c in range(0, IN_COLS, IN_CHUNK):
        acc = _dot(h_ref[...], w_ref[:, c:c + IN_CHUNK])
        if OFF_AQ <= c < OFF_AQ + ATT_W:
            acc = acc * ATT_Q_SCALE
        o_ref[:, c:c + IN_CHUNK] = acc.astype(BF16)


def _inproj_call(x2d, gain, w_bf16, tm):
    rows = x2d.shape[0]
    return pl.pallas_call(
        _inproj_kernel,
        out_shape=jax.ShapeDtypeStruct((rows, IN_COLS), BF16),
        grid=(rows // tm,),
        in_specs=[
            pl.BlockSpec((tm, D_MODEL), lambda i: (i, 0)),
            pl.BlockSpec((1, D_MODEL), lambda i: (0, 0)),
            pl.BlockSpec((D_MODEL, IN_COLS), lambda i: (0, 0), pipeline_mode=pl.Buffered(1)),
        ],
        out_specs=pl.BlockSpec((tm, IN_COLS), lambda i: (i, 0)),
        scratch_shapes=[pltpu.VMEM((tm, D_MODEL), BF16)],
        compiler_params=pltpu.CompilerParams(
            dimension_semantics=("parallel",), vmem_limit_bytes=VMEM_LIMIT),
        name="inproj",
    )(x2d, gain, w_bf16)


RET_C = 256
RET_BATCH = 2


def _retention_kernel(lg_ref, q_ref, k_ref, v_ref, km_ref, vm_ref, o_ref,
                      tq_ref, tk_ref, td_ref, sel_ref, ts_ref, tm_ref, u_ref, s_ref, i_ref,
                      *, n_chunks):
    pair = pl.program_id(0)
    C = RET_C
    k_scale = RET_DK ** -0.5

    @pl.when(pl.program_id(1) == 0)
    def _build_tables():
        lgf = [lg_ref[0, 2 * pair + hl] for hl in range(2)]
        lgb = [lg_ref[1, 2 * pair + hl] for hl in range(2)]
        row = lax.broadcasted_iota(jnp.int32, (C, LANES), 0).astype(F32)
        first = lax.broadcasted_iota(jnp.int32, (C, LANES), 1) < RET_DK
        pos_t = lax.broadcasted_iota(jnp.int32, (LANES, C), 1).astype(F32)
        first_t = lax.broadcasted_iota(jnp.int32, (LANES, C), 0) < RET_DK
        diff = (lax.broadcasted_iota(jnp.int32, (C, C), 0)
                - lax.broadcasted_iota(jnp.int32, (C, C), 1)).astype(F32)
        srow = lax.broadcasted_iota(jnp.int32, (LANES, 2 * RET_DV), 0)
        scol = lax.broadcasted_iota(jnp.int32, (LANES, 2 * RET_DV), 1)
        mrow = lax.broadcasted_iota(jnp.int32, (N_META, LANES), 0).astype(F32)
        mfirst = lax.broadcasted_iota(jnp.int32, (N_META, LANES), 1) < RET_DK

        def both(sel, fn, lg):
            return jnp.where(sel, fn(lg[0]), fn(lg[1]))

        tq_ref[0] = both(first, lambda l: jnp.exp((row + 1.0) * l), lgf)
        tq_ref[1] = both(first, lambda l: jnp.exp((C - row) * l), lgb)
        tk_ref[0] = both(first_t, lambda l: jnp.exp((C - 1.0 - pos_t) * l), lgf) * k_scale
        tk_ref[1] = both(first_t, lambda l: jnp.exp(pos_t * l), lgb) * k_scale
        tm_ref[...] = both(mfirst, lambda l: jnp.exp((N_META - 1.0 - mrow) * l), lgf) * k_scale
        for hl in range(2):
            td_ref[hl] = k_scale * jnp.where(diff >= 0, jnp.exp(jnp.maximum(diff, 0.0) * lgf[hl]),
                                             jnp.exp(jnp.maximum(-diff, 0.0) * lgb[hl]))
            sel_ref[hl] = jnp.where(first == (hl == 0), 1.0, 0.0).astype(BF16)
        own = (srow < RET_DK) == (scol < RET_DV)
        zeros = jnp.zeros((LANES, 2 * RET_DV), F32)
        ts_ref[0] = jnp.where(own, both(srow < RET_DK, lambda l: jnp.exp(zeros + C * l), lgf), 0.0)
        ts_ref[1] = jnp.where(own, both(srow < RET_DK, lambda l: jnp.exp(zeros + C * l), lgb), 0.0)
        ts_ref[2] = jnp.where(own, 1.0, 0.0)

    def head(x, hl):
        return x[:, hl * RET_DV:(hl + 1) * RET_DV]

    own_mask = ts_ref[2]
    for bb in range(RET_BATCH):
        _retention_one(q_ref.at[bb], k_ref.at[bb], v_ref.at[bb], km_ref, vm_ref, o_ref.at[bb],
                       tq_ref, tk_ref, td_ref, sel_ref, ts_ref, tm_ref, u_ref.at[bb], s_ref.at[bb],
                       i_ref.at[bb], own_mask, head, n_chunks)


def _retention_one(q_ref, k_ref, v_ref, km_ref, vm_ref, o_ref, tq_ref, tk_ref, td_ref, sel_ref,
                   ts_ref, tm_ref, u_ref, s_ref, i_ref, own_mask, head, n_chunks):
    C = RET_C

    for c in range(n_chunks):
        rows = slice(c * C, (c + 1) * C)
        kc = k_ref[rows, :]
        vc = v_ref[rows, :]
        k_t = kc.astype(F32).T
        lhs = jnp.concatenate([k_t * tk_ref[0], k_t * tk_ref[1]], axis=0).astype(BF16)
        u = _dot(lhs, vc)
        u_ref[c, 0:LANES, :] = u[0:LANES] * own_mask
        u_ref[c, LANES:2 * LANES, :] = u[LANES:2 * LANES] * own_mask
        q16 = q_ref[rows, :]
        for hl in range(2):
            s = _dot_nt(q16 * sel_ref[hl], kc) * td_ref[hl]
            i_ref[c, :, hl * RET_DV:(hl + 1) * RET_DV] = _dot(s.astype(BF16), head(vc, hl))

    sf = _dot_tn((km_ref[...].astype(F32) * tm_ref[...]).astype(BF16), vm_ref[...]) * own_mask
    for c in range(n_chunks):
        s_ref[c, 0:LANES, :] = sf.astype(BF16)
        if c + 1 < n_chunks:
            sf = sf * ts_ref[0] + u_ref[c, 0:LANES, :]
    sb = jnp.zeros((LANES, 2 * RET_DV), F32)
    for c in reversed(range(n_chunks)):
        s_ref[c, LANES:2 * LANES, :] = sb.astype(BF16)
        if c > 0:
            sb = sb * ts_ref[1] + u_ref[c, LANES:2 * LANES, :]

    for c in range(n_chunks):
        rows = slice(c * C, (c + 1) * C)
        qc = q_ref[rows, :].astype(F32)
        q_cross = jnp.concatenate([(qc * tq_ref[0]).astype(BF16),
                                   (qc * tq_ref[1]).astype(BF16)], axis=1)
        o_ref[rows, :] = (i_ref[c] + _dot(q_cross, s_ref[c])).astype(BF16)


def _retention_call(log_gammas, proj, proj_meta, batch, seq):
    assert seq % RET_C == 0
    n_chunks = seq // RET_C
    n_pairs = RET_HEADS // 2
    kernel = functools.partial(_retention_kernel, n_chunks=n_chunks)
    qb, kb = OFF_RQ // LANES, OFF_RK // LANES
    vb = OFF_RV // (2 * RET_DV)
    return pl.pallas_call(
        kernel,
        out_shape=jax.ShapeDtypeStruct((batch, seq, RET_W), BF16),
        grid=(n_pairs, batch // RET_BATCH),
        in_specs=[
            pl.BlockSpec(memory_space=pltpu.SMEM),
            pl.BlockSpec((RET_BATCH, seq, LANES), lambda p, b: (b, 0, qb + p)),
            pl.BlockSpec((RET_BATCH, seq, LANES), lambda p, b: (b, 0, kb + p)),
            pl.BlockSpec((RET_BATCH, seq, 2 * RET_DV), lambda p, b: (b, 0, vb + p)),
            pl.BlockSpec((N_META, LANES), lambda p, b: (0, kb + p)),
            pl.BlockSpec((N_META, 2 * RET_DV), lambda p, b: (0, vb + p)),
        ],
        out_specs=pl.BlockSpec((RET_BATCH, seq, 2 * RET_DV), lambda p, b: (b, 0, p)),
        scratch_shapes=[
            pltpu.VMEM((2, RET_C, LANES), F32),
            pltpu.VMEM((2, LANES, RET_C), F32),
            pltpu.VMEM((2, RET_C, RET_C), F32),
            pltpu.VMEM((2, RET_C, LANES), BF16),
            pltpu.VMEM((3, LANES, 2 * RET_DV), F32),
            pltpu.VMEM((N_META, LANES), F32),
            pltpu.VMEM((RET_BATCH, n_chunks, 2 * LANES, 2 * RET_DV), F32),
            pltpu.VMEM((RET_BATCH, n_chunks, 2 * LANES, 2 * RET_DV), BF16),
            pltpu.VMEM((RET_BATCH, n_chunks, RET_C, 2 * RET_DV), F32),
        ],
        compiler_params=pltpu.CompilerParams(
            dimension_semantics=("arbitrary", "arbitrary"), vmem_limit_bytes=VMEM_LIMIT),
        name="retention",
    )(log_gammas, proj, proj, proj, proj_meta, proj_meta)


def _pair_heads(w):
    w5 = w.reshape(ATT_KV_HEADS // 2, 2, ATT_GROUP, ATT_DH, w.shape[-1])
    return w5.transpose(0, 2, 1, 3, 4).reshape(w.shape)


QB = 64
BAND = QB + 2 * WINDOW
META_PAD = 64
N_KEYS = BAND + META_PAD
ATT_ROWS = 1024
N_OFFSETS = 2 * WINDOW // QB + 1
SINK_COL = BAND + N_META
HB_UNROLL = 8


def _attention_kernel(sink_ref, q_ref, k_ref, v_ref, km_ref, vm_ref, *rest, seq, n_cast):
    cast_src, o_ref = rest[:n_cast], rest[n_cast]
    cast_dst, bias_ref = rest[n_cast + 1:2 * n_cast + 1], rest[2 * n_cast + 1]
    for src, dst in zip(cast_src, cast_dst):
        dst[...] = src[...].astype(BF16)
    _attention_body(sink_ref, q_ref, k_ref, v_ref, km_ref, vm_ref, o_ref, bias_ref, seq=seq)


def _attention_body(sink_ref, q_ref, k_ref, v_ref, km_ref, vm_ref, o_ref, bias_ref, *, seq):
    n = pl.program_id(1)

    @pl.when((pl.program_id(0) == 0) & (n == 0))
    def _build_bias():
        r = lax.broadcasted_iota(jnp.int32, (QB, N_KEYS), 0)
        j = lax.broadcasted_iota(jnp.int32, (QB, N_KEYS), 1)
        for t in range(N_OFFSETS):
            dist = jnp.abs(j - t * QB - r)
            band_ok = (j < BAND) & (dist <= WINDOW)
            ndist = dist.astype(F32) * (-LOG2E)
            for head in range(ATT_Q_HEADS):
                slope = 2.0 ** (-8.0 * (head + 1) / ATT_Q_HEADS)
                sink = sink_ref[head] * LOG2E
                rest = jnp.where((j >= BAND) & (j < SINK_COL), 0.0,
                                 jnp.where(j == SINK_COL, sink, NEG_INF))
                bias_ref[t, head] = jnp.where(band_ok, slope * ndist, rest)

    low_k = lax.broadcasted_iota(jnp.int32, (N_KEYS, LANES), 1) < ATT_DH

    def block_diag(x):
        zero = jnp.zeros_like(x)
        return jnp.concatenate([jnp.where(low_k, x, zero), jnp.where(low_k, zero, x)], axis=0)

    ones_bd = jnp.concatenate([jnp.where(low_k, 1.0, 0.0), jnp.where(low_k, 0.0, 1.0)],
                              axis=0).astype(BF16)

    def body(it, carry):
        for u in range(HB_UNROLL):
            hb = it * HB_UNROLL + u
            row0 = n * ATT_ROWS + hb * QB
            start = pl.multiple_of(jnp.clip(row0 - WINDOW, 0, seq - BAND), QB)
            t = (row0 - start) // QB
            rows = pl.ds(pl.multiple_of(hb * QB, QB), QB)
            for jp in range(ATT_KV_HEADS // 2):
                ksl = slice(jp * LANES, (jp + 1) * LANES)
                k_bd = block_diag(jnp.concatenate(
                    [k_ref[pl.ds(start, BAND), ksl], km_ref[:, ksl]], axis=0))
                v_bd = jnp.concatenate([block_diag(jnp.concatenate(
                    [v_ref[pl.ds(start, BAND), ksl], vm_ref[:, ksl]], axis=0)), ones_bd], axis=1)
                groups = [slice((jp * ATT_GROUP + g) * LANES, (jp * ATT_GROUP + g + 1) * LANES)
                          for g in range(ATT_GROUP)]
                q_stack = jnp.concatenate([q_ref[rows, grp] for grp in groups], axis=0)
                s = _dot_nt(q_stack, k_bd)
                probs = []
                for g in range(ATT_GROUP):
                    p_halves = []
                    for half in range(2):
                        head = (2 * jp + half) * ATT_GROUP + g
                        sg = (s[g * QB:(g + 1) * QB, half * N_KEYS:(half + 1) * N_KEYS]
                              + bias_ref[t, head])
                        p = jnp.exp2(sg - jnp.max(sg, axis=-1, keepdims=True))
                        p_halves.append(p.astype(BF16))
                    probs.append(jnp.concatenate(p_halves, axis=1))
                o = _dot(jnp.concatenate(probs, axis=0), v_bd)
                out = o[:, 0:LANES] / o[:, LANES:2 * LANES]
                for g in range(ATT_GROUP):
                    o_ref[rows, groups[g]] = out[g * QB:(g + 1) * QB].astype(BF16)
        return carry

    lax.fori_loop(0, ATT_ROWS // QB // HB_UNROLL, body, 0)


def _paired_src_block(i, blocks_per_group):
    grp, sub = i // blocks_per_group, i % blocks_per_group
    j, g, half = grp // (2 * ATT_GROUP), (grp // 2) % ATT_GROUP, grp % 2
    return ((j * 2 + half) * ATT_GROUP + g) * blocks_per_group + sub


def _attention_call(sink, proj, k_meta, v_meta, w_casts, batch, seq):
    assert seq % ATT_ROWS == 0 and seq >= BAND
    n_blk = seq // ATT_ROWS
    steps = batch * n_blk
    kernel = functools.partial(_attention_kernel, seq=seq, n_cast=len(w_casts))
    kv_w = ATT_KV_HEADS * ATT_DH
    step = lambda b, n: b * n_blk + n
    cast_in, cast_out, cast_shapes = [], [], []
    for idx, w in enumerate(w_casts):
        rows, cols = w.shape
        reps = 1
        while rows % (steps // reps) or (rows // (steps // reps)) % 16:
            reps *= 2
        blk = rows // (steps // reps)
        dst_map = lambda b, n, reps=reps: (step(b, n) // reps, 0)
        if idx == 1:
            assert reps == 1 and ATT_DH % blk == 0
            src_map = lambda b, n, bpg=ATT_DH // blk: (_paired_src_block(step(b, n), bpg), 0)
        else:
            src_map = dst_map
        cast_in.append(pl.BlockSpec((blk, cols), src_map))
        cast_out.append(pl.BlockSpec((blk, cols), dst_map))
        cast_shapes.append(jax.ShapeDtypeStruct((rows, cols), BF16))
    return pl.pallas_call(
        kernel,
        out_shape=[jax.ShapeDtypeStruct((batch, seq, ATT_W), BF16)] + cast_shapes,
        grid=(batch, n_blk),
        in_specs=[
            pl.BlockSpec(memory_space=pltpu.SMEM),
            pl.BlockSpec((None, ATT_ROWS, ATT_W), lambda b, n: (b, n, OFF_AQ // ATT_W)),
            pl.BlockSpec((None, seq, kv_w), lambda b, n: (b, 0, OFF_AK // kv_w)),
            pl.BlockSpec((None, seq, kv_w), lambda b, n: (b, 0, OFF_AV // kv_w)),
            pl.BlockSpec((META_PAD, kv_w), lambda b, n: (0, 0)),
            pl.BlockSpec((META_PAD, kv_w), lambda b, n: (0, 0)),
        ] + cast_in,
        out_specs=[pl.BlockSpec((None, ATT_ROWS, ATT_W), lambda b, n: (b, n, 0))] + cast_out,
        scratch_shapes=[pltpu.VMEM((N_OFFSETS, ATT_Q_HEADS, QB, N_KEYS), F32)],
        compiler_params=pltpu.CompilerParams(
            dimension_semantics=("arbitrary", "arbitrary"), vmem_limit_bytes=VMEM_LIMIT),
        name="attention",
    )(sink, proj, proj, proj, k_meta, v_meta, *w_casts)


FF_CHUNK = 256


def _mixffn_kernel(x_ref, ret_ref, rg_ref, gain_ref, att_ref, ga_ref, gb_ref, wr_ref, wa_ref, wo_ref,
                   nf_ref, wg_ref, wu_ref, wd_ref, nfin_ref, o_ref, r_ref, h_ref, act_ref):
    gb = gb_ref[...].astype(F32)
    att_part = _sigmoid(gb) * _dot(att_ref[...], wa_ref[...])
    for h in range(RET_HEADS):
        sl = slice(h * RET_DV, (h + 1) * RET_DV)
        o = ret_ref[:, sl].astype(F32)
        d = o - jnp.mean(o, axis=-1, keepdims=True)
        var = jnp.mean(d * d, axis=-1, keepdims=True)
        g = rg_ref[:, sl].astype(F32)
        r_ref[:, sl] = (g * _sigmoid(g) * (d * lax.rsqrt(var + GN_EPS) * gain_ref[:, sl])).astype(BF16)
    ga = ga_ref[...].astype(F32)
    merged = _sigmoid(ga) * _dot(r_ref[...], wr_ref[...]) + att_part
    x = x_ref[...] + _dot(merged.astype(BF16), wo_ref[...])
    ms = jnp.mean(x * x, axis=-1, keepdims=True)
    h_ref[...] = (x * lax.rsqrt(ms + RMS_EPS) * nf_ref[...]).astype(BF16)
    for c in range(0, D_FF, FF_CHUNK):
        a = _dot(h_ref[...], wg_ref[:, c:c + FF_CHUNK])
        b = _dot(h_ref[...], wu_ref[:, c:c + FF_CHUNK])
        act_ref[:, c:c + FF_CHUNK] = (a * _sigmoid(a) * b).astype(BF16)
    y = x + _dot(act_ref[...], wd_ref[...])
    ms = jnp.mean(y * y, axis=-1, keepdims=True)
    o_ref[...] = y * lax.rsqrt(ms + RMS_EPS) * nfin_ref[...]


def _mixffn_call(x2d, ret, gn_gain, att, proj, wr, wa, wo, norm_ffn, wgu, wd, norm_final, tm):
    rows = x2d.shape[0]
    const = lambda shape, col=0: pl.BlockSpec(shape, lambda i: (0, col),
                                              pipeline_mode=pl.Buffered(1))
    row_tile = lambda col: pl.BlockSpec((tm, D_MODEL), lambda i: (i, col))
    return pl.pallas_call(
        _mixffn_kernel,
        out_shape=jax.ShapeDtypeStruct((rows, D_MODEL), F32),
        grid=(rows // tm,),
        in_specs=[
            row_tile(0), row_tile(0), row_tile(OFF_RG // RET_W), const((1, RET_W)), row_tile(0),
            row_tile(OFF_GA // D_MODEL), row_tile(OFF_GB // D_MODEL),
            const((RET_W, D_MODEL)), const((ATT_W, D_MODEL)), const((D_MODEL, D_MODEL)),
            const((1, D_MODEL)),
            const((D_MODEL, D_FF), 0), const((D_MODEL, D_FF), 1), const((D_FF, D_MODEL)),
            const((1, D_MODEL)),
        ],
        out_specs=row_tile(0),
        scratch_shapes=[pltpu.VMEM((tm, RET_W), BF16), pltpu.VMEM((tm, D_MODEL), BF16),
                        pltpu.VMEM((tm, D_FF), BF16)],
        compiler_params=pltpu.CompilerParams(
            dimension_semantics=("parallel",), vmem_limit_bytes=VMEM_LIMIT),
        name="mixffn",
    )(x2d, ret, proj, gn_gain, att, proj, proj, wr, wa, wo, norm_ffn, wgu, wgu, wd, norm_final)


def kernel(x, meta_tokens, w_in, ret_decay_logit_fwd, ret_decay_logit_bwd, ret_gn_gain, attn_sink,
           w_branch_ret, w_branch_att, w_out, norm_mix, norm_ffn, w_gate_up, w_down, norm_final):
    batch, seq, d = x.shape
    assert d == D_MODEL and seq % CHUNK == 0 and w_in.shape[0] == 1

    w_in0 = w_in[0]
    w_in_b = jnp.concatenate(
        [w_in0[:, :SRC_AQ], _pair_heads(w_in0[:, SRC_AQ:SRC_AK].T).T, w_in0[:, SRC_GA:],
         w_in0[:, SRC_AK:SRC_GA]], axis=1).astype(BF16)
    nmix = norm_mix[0].reshape(1, D_MODEL)
    nffn = norm_ffn[0].reshape(1, D_MODEL)
    nfin = norm_final.reshape(1, D_MODEL)

    x2d = x.reshape(batch * seq, D_MODEL)
    proj = _inproj_call(x2d, nmix, w_in_b, 1024)
    proj_meta = _inproj_call(meta_tokens.astype(x.dtype), nmix, w_in_b, N_META)
    proj3 = proj.reshape(batch, seq, IN_COLS)

    log_gammas = jnp.stack([jax.nn.log_sigmoid(ret_decay_logit_fwd[0].astype(F32)),
                            jax.nn.log_sigmoid(ret_decay_logit_bwd[0].astype(F32))])
    ret = _retention_call(log_gammas, proj3, proj_meta, batch, seq)

    kv_w = ATT_KV_HEADS * ATT_DH
    pad = ((0, META_PAD - N_META), (0, 0))
    k_meta = jnp.pad(proj_meta[:, OFF_AK:OFF_AK + kv_w], pad)
    v_meta = jnp.pad(proj_meta[:, OFF_AV:OFF_AV + kv_w], pad)
    att, wr, wa, wo, wgu, wd = _attention_call(
        attn_sink[0].astype(F32), proj3, k_meta, v_meta,
        (w_branch_ret[0], w_branch_att[0], w_out[0], w_gate_up[0], w_down[0]), batch, seq)

    out = _mixffn_call(x2d, ret.reshape(batch * seq, RET_W), ret_gn_gain[0].reshape(1, RET_W),
                       att.reshape(batch * seq, ATT_W), proj,
                       wr, wa, wo, nffn, wgu, wd, nfin, 512)
    return out.reshape(batch, seq, D_MODEL)
```

```python
import functools

import numpy as np
import jax
import jax.numpy as jnp
from jax import lax
from jax.experimental import pallas as pl
from jax.experimental.pallas import tpu as pltpu

D_MODEL = 1024
N_META = 16
RET_HEADS = 8
RET_DK = 64
RET_DV = 128
RET_W = RET_HEADS * RET_DV
ATT_Q_HEADS = 16
ATT_KV_HEADS = 4
ATT_GROUP = ATT_Q_HEADS // ATT_KV_HEADS
ATT_DH = 64
ATT_W = ATT_Q_HEADS * ATT_DH
WINDOW = 128
D_FF = 2816
IN_SIZES = (512, 512, 1024, 1024, 1024, 256, 256, 1024, 1024)
IN_COLS = sum(IN_SIZES)
OFF_RQ, OFF_RK, OFF_RV, OFF_RG, OFF_AQ, OFF_AK, OFF_AV, OFF_GA, OFF_GB = (
    int(c) for c in np.cumsum((0,) + IN_SIZES[:-1]))
RMS_EPS = 1e-6
GN_EPS = 1e-5
NEG_INF = -1e30
LOG2E = 1.4426950408889634
ATT_Q_SCALE = ATT_DH ** -0.5 * LOG2E

LANES = 128
VMEM_LIMIT = 56 * 1024 * 1024

F32 = jnp.float32
BF16 = jnp.bfloat16


def _sigmoid(x):
    return 1.0 / (1.0 + jnp.exp(-x))


def _dot(a, b):
    return jnp.dot(a, b, preferred_element_type=F32)


def _dot_nt(a, b):
    return lax.dot_general(a, b, (((1,), (1,)), ((), ())), preferred_element_type=F32)


def _dot_tn(a, b):
    return lax.dot_general(a, b, (((0,), (0,)), ((), ())), preferred_element_type=F32)


IN_CHUNK = 512
IN_SUB = 512


def _inproj_kernel(x_ref, g_ref, w_ref, wq_ref, o_ref, h_ref):
    tm = x_ref.shape[0]
    sub = min(IN_SUB, tm)
    for r0 in range(0, tm, sub):
        rows = slice(r0, r0 + sub)
        x = x_ref[rows, :]
        ms = jnp.mean(x * x, axis=-1, keepdims=True)
        h_ref[rows, :] = (x * lax.rsqrt(ms + RMS_EPS) * g_ref[...]).astype(BF16)
        for c in range(0, IN_COLS, IN_CHUNK):
            if OFF_AQ <= c < OFF_AQ + ATT_W:
                q0 = c - OFF_AQ
                acc = _dot(h_ref[rows, :], wq_ref[:, q0:q0 + IN_CHUNK]) * ATT_Q_SCALE
            else:
                acc = _dot(h_ref[rows, :], w_ref[:, c:c + IN_CHUNK])
            o_ref[rows, c:c + IN_CHUNK] = acc.astype(BF16)


def _inproj_call(x2d, gain, w_bf16, wq_bf16, tm):
    rows = x2d.shape[0]
    const = lambda shape: pl.BlockSpec(shape, lambda i: (0, 0), pipeline_mode=pl.Buffered(1))
    return pl.pallas_call(
        _inproj_kernel,
        out_shape=jax.ShapeDtypeStruct((rows, IN_COLS), BF16),
        grid=(rows // tm,),
        in_specs=[
            pl.BlockSpec((tm, D_MODEL), lambda i: (i, 0)),
            pl.BlockSpec((1, D_MODEL), lambda i: (0, 0)),
            const((D_MODEL, IN_COLS)), const((D_MODEL, ATT_W)),
        ],
        out_specs=pl.BlockSpec((tm, IN_COLS), lambda i: (i, 0)),
        scratch_shapes=[pltpu.VMEM((tm, D_MODEL), BF16)],
        compiler_params=pltpu.CompilerParams(
            dimension_semantics=("parallel",), vmem_limit_bytes=VMEM_LIMIT),
        name="inproj",
    )(x2d, gain, w_bf16, wq_bf16)


RET_C = 256
RET_BATCH = 2


def _retention_kernel(lg_ref, q_ref, k_ref, v_ref, km_ref, vm_ref, o_ref,
                      tq_ref, tk_ref, td_ref, sel_ref, ts_ref, tm_ref, u_ref, s_ref, i_ref,
                      *, n_chunks):
    pair = pl.program_id(0)
    C = RET_C
    k_scale = RET_DK ** -0.5

    @pl.when(pl.program_id(1) == 0)
    def _build_tables():
        lgf = [lg_ref[0, 2 * pair + hl] for hl in range(2)]
        lgb = [lg_ref[1, 2 * pair + hl] for hl in range(2)]
        row = lax.broadcasted_iota(jnp.int32, (C, LANES), 0).astype(F32)
        first = lax.broadcasted_iota(jnp.int32, (C, LANES), 1) < RET_DK
        pos_t = lax.broadcasted_iota(jnp.int32, (LANES, C), 1).astype(F32)
        first_t = lax.broadcasted_iota(jnp.int32, (LANES, C), 0) < RET_DK
        diff = (lax.broadcasted_iota(jnp.int32, (C, C), 0)
                - lax.broadcasted_iota(jnp.int32, (C, C), 1)).astype(F32)
        srow = lax.broadcasted_iota(jnp.int32, (LANES, 2 * RET_DV), 0)
        scol = lax.broadcasted_iota(jnp.int32, (LANES, 2 * RET_DV), 1)
        mrow = lax.broadcasted_iota(jnp.int32, (N_META, LANES), 0).astype(F32)
        mfirst = lax.broadcasted_iota(jnp.int32, (N_META, LANES), 1) < RET_DK

        def both(sel, fn, lg):
            return jnp.where(sel, fn(lg[0]), fn(lg[1]))

        tq_ref[0] = both(first, lambda l: jnp.exp((row + 1.0) * l), lgf)
        tq_ref[1] = both(first, lambda l: jnp.exp((C - row) * l), lgb)
        tk_ref[0] = both(first_t, lambda l: jnp.exp((C - 1.0 - pos_t) * l), lgf) * k_scale
        tk_ref[1] = both(first_t, lambda l: jnp.exp(pos_t * l), lgb) * k_scale
        tm_ref[...] = both(mfirst, lambda l: jnp.exp((N_META - 1.0 - mrow) * l), lgf) * k_scale
        for hl in range(2):
            td_ref[hl] = k_scale * jnp.where(diff >= 0, jnp.exp(jnp.maximum(diff, 0.0) * lgf[hl]),
                                             jnp.exp(jnp.maximum(-diff, 0.0) * lgb[hl]))
            sel_ref[hl] = jnp.where(first == (hl == 0), 1.0, 0.0).astype(BF16)
        own = (srow < RET_DK) == (scol < RET_DV)
        zeros = jnp.zeros((LANES, 2 * RET_DV), F32)
        ts_ref[0] = jnp.where(own, both(srow < RET_DK, lambda l: jnp.exp(zeros + C * l), lgf), 0.0)
        ts_ref[1] = jnp.where(own, both(srow < RET_DK, lambda l: jnp.exp(zeros + C * l), lgb), 0.0)
        ts_ref[2] = jnp.where(own, 1.0, 0.0)

    def head(x, hl):
        return x[:, hl * RET_DV:(hl + 1) * RET_DV]

    own_mask = ts_ref[2]
    for bb in range(RET_BATCH):
        _retention_one(q_ref.at[bb], k_ref.at[bb], v_ref.at[bb], km_ref, vm_ref, o_ref.at[bb],
                       tq_ref, tk_ref, td_ref, sel_ref, ts_ref, tm_ref, u_ref.at[bb], s_ref.at[bb],
                       i_ref.at[bb], own_mask, head, n_chunks)


def _retention_one(q_ref, k_ref, v_ref, km_ref, vm_ref, o_ref, tq_ref, tk_ref, td_ref, sel_ref,
                   ts_ref, tm_ref, u_ref, s_ref, i_ref, own_mask, head, n_chunks):
    C = RET_C

    for c in range(n_chunks):
        rows = slice(c * C, (c + 1) * C)
        kc = k_ref[rows, :]
        vc = v_ref[rows, :]
        k_t = kc.astype(F32).T
        lhs = jnp.concatenate([k_t * tk_ref[0], k_t * tk_ref[1]], axis=0).astype(BF16)
        u = _dot(lhs, vc)
        u_ref[c, 0:LANES, :] = u[0:LANES] * own_mask
        u_ref[c, LANES:2 * LANES, :] = u[LANES:2 * LANES] * own_mask
        q16 = q_ref[rows, :]
        for hl in range(2):
            s = _dot_nt(q16 * sel_ref[hl], kc) * td_ref[hl]
            i_ref[c, :, hl * RET_DV:(hl + 1) * RET_DV] = _dot(s.astype(BF16), head(vc, hl))

    sf = _dot_tn((km_ref[...].astype(F32) * tm_ref[...]).astype(BF16), vm_ref[...]) * own_mask
    for c in range(n_chunks):
        s_ref[c, 0:LANES, :] = sf.astype(BF16)
        if c + 1 < n_chunks:
            sf = sf * ts_ref[0] + u_ref[c, 0:LANES, :]
    sb = jnp.zeros((LANES, 2 * RET_DV), F32)
    for c in reversed(range(n_chunks)):
        s_ref[c, LANES:2 * LANES, :] = sb.astype(BF16)
        if c > 0:
            sb = sb * ts_ref[1] + u_ref[c, LANES:2 * LANES, :]

    for c in range(n_chunks):
        rows = slice(c * C, (c + 1) * C)
        qc = q_ref[rows, :].astype(F32)
        q_cross = jnp.concatenate([(qc * tq_ref[0]).astype(BF16),
                                   (qc * tq_ref[1]).astype(BF16)], axis=1)
        o_ref[rows, :] = (i_ref[c] + _dot(q_cross, s_ref[c])).astype(BF16)


def _retention_call(log_gammas, proj, proj_meta, batch, seq):
    assert seq % RET_C == 0 and batch % RET_BATCH == 0
    n_chunks = seq // RET_C
    n_pairs = RET_HEADS // 2
    kernel = functools.partial(_retention_kernel, n_chunks=n_chunks)
    qb, kb = OFF_RQ // LANES, OFF_RK // LANES
    vb = OFF_RV // (2 * RET_DV)
    return pl.pallas_call(
        kernel,
        out_shape=jax.ShapeDtypeStruct((batch, seq, RET_W), BF16),
        grid=(n_pairs, batch // RET_BATCH),
        in_specs=[
            pl.BlockSpec(memory_space=pltpu.SMEM),
            pl.BlockSpec((RET_BATCH, seq, LANES), lambda p, b: (b, 0, qb + p)),
            pl.BlockSpec((RET_BATCH, seq, LANES), lambda p, b: (b, 0, kb + p)),
            pl.BlockSpec((RET_BATCH, seq, 2 * RET_DV), lambda p, b: (b, 0, vb + p)),
            pl.BlockSpec((N_META, LANES), lambda p, b: (0, kb + p)),
            pl.BlockSpec((N_META, 2 * RET_DV), lambda p, b: (0, vb + p)),
        ],
        out_specs=pl.BlockSpec((RET_BATCH, seq, 2 * RET_DV), lambda p, b: (b, 0, p)),
        scratch_shapes=[
            pltpu.VMEM((2, RET_C, LANES), F32),
            pltpu.VMEM((2, LANES, RET_C), F32),
            pltpu.VMEM((2, RET_C, RET_C), F32),
            pltpu.VMEM((2, RET_C, LANES), BF16),
            pltpu.VMEM((3, LANES, 2 * RET_DV), F32),
            pltpu.VMEM((N_META, LANES), F32),
            pltpu.VMEM((RET_BATCH, n_chunks, 2 * LANES, 2 * RET_DV), F32),
            pltpu.VMEM((RET_BATCH, n_chunks, 2 * LANES, 2 * RET_DV), BF16),
            pltpu.VMEM((RET_BATCH, n_chunks, RET_C, 2 * RET_DV), F32),
        ],
        compiler_params=pltpu.CompilerParams(
            dimension_semantics=("arbitrary", "arbitrary"), vmem_limit_bytes=VMEM_LIMIT),
        name="retention",
    )(log_gammas, proj, proj, proj, proj_meta, proj_meta)


def _pair_heads(w):
    w5 = w.reshape(ATT_KV_HEADS // 2, 2, ATT_GROUP, ATT_DH, w.shape[-1])
    return w5.transpose(0, 2, 1, 3, 4).reshape(w.shape)


QB = 64
BAND = QB + 2 * WINDOW
META_PAD = 64
N_KEYS = BAND + META_PAD
ATT_ROWS = 1024
N_OFFSETS = 2 * WINDOW // QB + 1
SINK_COL = BAND + N_META
HB_UNROLL = 8


def _attention_kernel(sink_ref, q_ref, k_ref, v_ref, km_ref, vm_ref, *rest, seq, n_cast):
    cast_src, o_ref = rest[:n_cast], rest[n_cast]
    cast_dst, bias_ref = rest[n_cast + 1:2 * n_cast + 1], rest[2 * n_cast + 1]
    for src, dst in zip(cast_src, cast_dst):
        dst[...] = src[...].astype(BF16)
    _attention_body(sink_ref, q_ref, k_ref, v_ref, km_ref, vm_ref, o_ref, bias_ref, seq=seq)


def _attention_body(sink_ref, q_ref, k_ref, v_ref, km_ref, vm_ref, o_ref, bias_ref, *, seq):
    n = pl.program_id(1)

    @pl.when((pl.program_id(0) == 0) & (n == 0))
    def _build_bias():
        r = lax.broadcasted_iota(jnp.int32, (QB, N_KEYS), 0)
        j = lax.broadcasted_iota(jnp.int32, (QB, N_KEYS), 1)
        for t in range(N_OFFSETS):
            dist = jnp.abs(j - t * QB - r)
            band_ok = (j < BAND) & (dist <= WINDOW)
            ndist = dist.astype(F32) * (-LOG2E)
            for head in range(ATT_Q_HEADS):
                slope = 2.0 ** (-8.0 * (head + 1) / ATT_Q_HEADS)
                sink = sink_ref[head] * LOG2E
                rest = jnp.where((j >= BAND) & (j < SINK_COL), 0.0,
                                 jnp.where(j == SINK_COL, sink, NEG_INF))
                bias_ref[t, head] = jnp.where(band_ok, slope * ndist, rest)

    low_k = lax.broadcasted_iota(jnp.int32, (N_KEYS, LANES), 1) < ATT_DH

    def block_diag(x):
        zero = jnp.zeros_like(x)
        return jnp.concatenate([jnp.where(low_k, x, zero), jnp.where(low_k, zero, x)], axis=0)

    ones_bd = jnp.concatenate([jnp.where(low_k, 1.0, 0.0), jnp.where(low_k, 0.0, 1.0)],
                              axis=0).astype(BF16)
    meta_fill = jnp.zeros((META_PAD - N_META, LANES), BF16)

    def body(it, carry):
        for u in range(HB_UNROLL):
            hb = it * HB_UNROLL + u
            row0 = n * ATT_ROWS + hb * QB
            start = pl.multiple_of(jnp.clip(row0 - WINDOW, 0, seq - BAND), QB)
            t = (row0 - start) // QB
            rows = pl.ds(pl.multiple_of(hb * QB, QB), QB)
            for jp in range(ATT_KV_HEADS // 2):
                ksl = slice(jp * LANES, (jp + 1) * LANES)
                k_bd = block_diag(jnp.concatenate(
                    [k_ref[pl.ds(start, BAND), ksl], km_ref[:, ksl], meta_fill], axis=0))
                v_bd = jnp.concatenate([block_diag(jnp.concatenate(
                    [v_ref[pl.ds(start, BAND), ksl], vm_ref[:, ksl], meta_fill], axis=0)),
                    ones_bd], axis=1)
                groups = [slice((jp * ATT_GROUP + g) * LANES, (jp * ATT_GROUP + g + 1) * LANES)
                          for g in range(ATT_GROUP)]
                q_stack = jnp.concatenate([q_ref[rows, grp] for grp in groups], axis=0)
                s = _dot_nt(q_stack, k_bd)
                probs = []
                for g in range(ATT_GROUP):
                    p_halves = []
                    for half in range(2):
                        head = (2 * jp + half) * ATT_GROUP + g
                        sg = (s[g * QB:(g + 1) * QB, half * N_KEYS:(half + 1) * N_KEYS]
                              + bias_ref[t, head])
                        p = jnp.exp2(sg - jnp.max(sg, axis=-1, keepdims=True))
                        p_halves.append(p.astype(BF16))
                    probs.append(jnp.concatenate(p_halves, axis=1))
                o = _dot(jnp.concatenate(probs, axis=0), v_bd)
                out = o[:, 0:LANES] / o[:, LANES:2 * LANES]
                for g in range(ATT_GROUP):
                    o_ref[rows, groups[g]] = out[g * QB:(g + 1) * QB].astype(BF16)
        return carry

    lax.fori_loop(0, ATT_ROWS // QB // HB_UNROLL, body, 0)


def _paired_src_block(i, blocks_per_group):
    grp, sub = i // blocks_per_group, i % blocks_per_group
    j, g, half = grp // (2 * ATT_GROUP), (grp // 2) % ATT_GROUP, grp % 2
    return ((j * 2 + half) * ATT_GROUP + g) * blocks_per_group + sub


def _attention_call(sink, proj, proj_meta, w_casts, batch, seq):
    assert seq % ATT_ROWS == 0 and seq >= BAND
    n_blk = seq // ATT_ROWS
    steps = batch * n_blk
    kernel = functools.partial(_attention_kernel, seq=seq, n_cast=len(w_casts))
    kv_w = ATT_KV_HEADS * ATT_DH
    step = lambda b, n: b * n_blk + n
    cast_in, cast_out, cast_shapes = [], [], []
    for idx, w in enumerate(w_casts):
        rows, cols = w.shape
        reps = 1
        while rows % (steps // reps) or (rows // (steps // reps)) % 16:
            reps *= 2
        blk = rows // (steps // reps)
        dst_map = lambda b, n, reps=reps: (step(b, n) // reps, 0)
        if idx == 1:
            assert reps == 1 and ATT_DH % blk == 0
            src_map = lambda b, n, bpg=ATT_DH // blk: (_paired_src_block(step(b, n), bpg), 0)
        else:
            src_map = dst_map
        cast_in.append(pl.BlockSpec((blk, cols), src_map))
        cast_out.append(pl.BlockSpec((blk, cols), dst_map))
        cast_shapes.append(jax.ShapeDtypeStruct((rows, cols), BF16))
    return pl.pallas_call(
        kernel,
        out_shape=[jax.ShapeDtypeStruct((batch, seq, ATT_W), BF16)] + cast_shapes,
        grid=(batch, n_blk),
        in_specs=[
            pl.BlockSpec(memory_space=pltpu.SMEM),
            pl.BlockSpec((None, ATT_ROWS, ATT_W), lambda b, n: (b, n, OFF_AQ // ATT_W)),
            pl.BlockSpec((None, seq, kv_w), lambda b, n: (b, 0, OFF_AK // kv_w)),
            pl.BlockSpec((None, seq, kv_w), lambda b, n: (b, 0, OFF_AV // kv_w)),
            pl.BlockSpec((N_META, kv_w), lambda b, n: (0, OFF_AK // kv_w)),
            pl.BlockSpec((N_META, kv_w), lambda b, n: (0, OFF_AV // kv_w)),
        ] + cast_in,
        out_specs=[pl.BlockSpec((None, ATT_ROWS, ATT_W), lambda b, n: (b, n, 0))] + cast_out,
        scratch_shapes=[pltpu.VMEM((N_OFFSETS, ATT_Q_HEADS, QB, N_KEYS), F32)],
        compiler_params=pltpu.CompilerParams(
            dimension_semantics=("arbitrary", "arbitrary"), vmem_limit_bytes=VMEM_LIMIT),
        name="attention",
    )(sink, proj, proj, proj, proj_meta, proj_meta, *w_casts)


FF_CHUNK = 256
GATE_HALF = D_MODEL // 2


def _mixffn_kernel(x_ref, ret_ref, rg_ref, gain_ref, att_ref, ga0_ref, ga1_ref, gb0_ref, gb1_ref,
                   wr_ref, wa_ref, wo_ref, nf_ref, wg_ref, wu_ref, wd_ref, nfin_ref, o_ref,
                   r_ref, h_ref, act_ref):
    gb = jnp.concatenate([gb0_ref[...], gb1_ref[...]], axis=1).astype(F32)
    att_part = _sigmoid(gb) * _dot(att_ref[...], wa_ref[...])
    for h in range(RET_HEADS):
        sl = slice(h * RET_DV, (h + 1) * RET_DV)
        o = ret_ref[:, sl].astype(F32)
        d = o - jnp.mean(o, axis=-1, keepdims=True)
        var = jnp.mean(d * d, axis=-1, keepdims=True)
        g = rg_ref[:, sl].astype(F32)
        r_ref[:, sl] = (g * _sigmoid(g) * (d * lax.rsqrt(var + GN_EPS) * gain_ref[:, sl])).astype(BF16)
    ga = jnp.concatenate([ga0_ref[...], ga1_ref[...]], axis=1).astype(F32)
    merged = _sigmoid(ga) * _dot(r_ref[...], wr_ref[...]) + att_part
    x = x_ref[...] + _dot(merged.astype(BF16), wo_ref[...])
    ms = jnp.mean(x * x, axis=-1, keepdims=True)
    h_ref[...] = (x * lax.rsqrt(ms + RMS_EPS) * nf_ref[...]).astype(BF16)
    for c in range(0, D_FF, FF_CHUNK):
        a = _dot(h_ref[...], wg_ref[:, c:c + FF_CHUNK])
        b = _dot(h_ref[...], wu_ref[:, c:c + FF_CHUNK])
        act_ref[:, c:c + FF_CHUNK] = (a * _sigmoid(a) * b).astype(BF16)
    y = x + _dot(act_ref[...], wd_ref[...])
    ms = jnp.mean(y * y, axis=-1, keepdims=True)
    o_ref[...] = y * lax.rsqrt(ms + RMS_EPS) * nfin_ref[...]


def _mixffn_call(x2d, ret, gn_gain, att, proj, wr, wa, wo, norm_ffn, wgu, wd, norm_final, tm):
    rows = x2d.shape[0]
    const = lambda shape, col=0: pl.BlockSpec(shape, lambda i: (0, col),
                                              pipeline_mode=pl.Buffered(1))
    row_tile = lambda col: pl.BlockSpec((tm, D_MODEL), lambda i: (i, col))
    half_tile = lambda col: pl.BlockSpec((tm, GATE_HALF), lambda i: (i, col))
    return pl.pallas_call(
        _mixffn_kernel,
        out_shape=jax.ShapeDtypeStruct((rows, D_MODEL), F32),
        grid=(rows // tm,),
        in_specs=[
            row_tile(0), row_tile(0), row_tile(OFF_RG // RET_W), const((1, RET_W)), row_tile(0),
            half_tile(OFF_GA // GATE_HALF), half_tile(OFF_GA // GATE_HALF + 1),
            half_tile(OFF_GB // GATE_HALF), half_tile(OFF_GB // GATE_HALF + 1),
            const((RET_W, D_MODEL)), const((ATT_W, D_MODEL)), const((D_MODEL, D_MODEL)),
            const((1, D_MODEL)),
            const((D_MODEL, D_FF), 0), const((D_MODEL, D_FF), 1), const((D_FF, D_MODEL)),
            const((1, D_MODEL)),
        ],
        out_specs=row_tile(0),
        scratch_shapes=[pltpu.VMEM((tm, RET_W), BF16), pltpu.VMEM((tm, D_MODEL), BF16),
                        pltpu.VMEM((tm, D_FF), BF16)],
        compiler_params=pltpu.CompilerParams(
            dimension_semantics=("parallel",), vmem_limit_bytes=VMEM_LIMIT),
        name="mixffn",
    )(x2d, ret, proj, gn_gain, att, proj, proj, proj, proj, wr, wa, wo, norm_ffn, wgu, wgu, wd,
      norm_final)


def kernel(x, meta_tokens, w_in, ret_decay_logit_fwd, ret_decay_logit_bwd, ret_gn_gain, attn_sink,
           w_branch_ret, w_branch_att, w_out, norm_mix, norm_ffn, w_gate_up, w_down, norm_final):
    batch, seq, d = x.shape
    assert d == D_MODEL and w_in.shape[0] == 1

    w_in0 = w_in[0]
    w_in_b = w_in0.astype(BF16)
    w_q_b = _pair_heads(w_in0[:, OFF_AQ:OFF_AK].T).T.astype(BF16)
    nmix = norm_mix[0].reshape(1, D_MODEL)
    nffn = norm_ffn[0].reshape(1, D_MODEL)
    nfin = norm_final.reshape(1, D_MODEL)

    x2d = x.reshape(batch * seq, D_MODEL)
    proj = _inproj_call(x2d, nmix, w_in_b, w_q_b, 1024)
    proj_meta = _inproj_call(meta_tokens.astype(x.dtype), nmix, w_in_b, w_q_b, N_META)
    proj3 = proj.reshape(batch, seq, IN_COLS)

    log_gammas = jnp.stack([jax.nn.log_sigmoid(ret_decay_logit_fwd[0].astype(F32)),
                            jax.nn.log_sigmoid(ret_decay_logit_bwd[0].astype(F32))])
    ret = _retention_call(log_gammas, proj3, proj_meta, batch, seq)

    att, wr, wa, wo, wgu, wd = _attention_call(
        attn_sink[0].astype(F32), proj3, proj_meta,
        (w_branch_ret[0], w_branch_att[0], w_out[0], w_gate_up[0], w_down[0]), batch, seq)

    out = _mixffn_call(x2d, ret.reshape(batch * seq, RET_W), ret_gn_gain[0].reshape(1, RET_W),
                       att.reshape(batch * seq, ATT_W), proj,
                       wr, wa, wo, nffn, wgu, wd, nfin, 512)
    return out.reshape(batch, seq, D_MODEL)
```

```python
import functools

import numpy as np
import jax
import jax.numpy as jnp
from jax import lax
from jax.experimental import pallas as pl
from jax.experimental.pallas import tpu as pltpu

D_MODEL = 1024
N_META = 16
RET_HEADS = 8
RET_DK = 64
RET_DV = 128
RET_W = RET_HEADS * RET_DV
ATT_Q_HEADS = 16
ATT_KV_HEADS = 4
ATT_GROUP = ATT_Q_HEADS // ATT_KV_HEADS
ATT_DH = 64
ATT_W = ATT_Q_HEADS * ATT_DH
WINDOW = 128
D_FF = 2816
IN_SIZES = (512, 512, 1024, 1024, 1024, 256, 256, 1024, 1024)
IN_COLS = sum(IN_SIZES)
OFF_RQ, OFF_RK, OFF_RV, OFF_RG, OFF_AQ, OFF_AK, OFF_AV, OFF_GA, OFF_GB = (
    int(c) for c in np.cumsum((0,) + IN_SIZES[:-1]))
RMS_EPS = 1e-6
GN_EPS = 1e-5
NEG_INF = -1e30
LOG2E = 1.4426950408889634
ATT_Q_SCALE = ATT_DH ** -0.5 * LOG2E

LANES = 128
VMEM_LIMIT = 56 * 1024 * 1024

F32 = jnp.float32
BF16 = jnp.bfloat16


def _sigmoid(x):
    return 1.0 / (1.0 + jnp.exp(-x))


def _dot(a, b):
    return jnp.dot(a, b, preferred_element_type=F32)


def _dot_nt(a, b):
    return lax.dot_general(a, b, (((1,), (1,)), ((), ())), preferred_element_type=F32)


def _dot_tn(a, b):
    return lax.dot_general(a, b, (((0,), (0,)), ((), ())), preferred_element_type=F32)


IN_CHUNK = 512
IN_SUB = 512


def _inproj_kernel(x_ref, g_ref, w_ref, wq_ref, o_ref, h_ref):
    tm = x_ref.shape[0]
    sub = min(IN_SUB, tm)
    for r0 in range(0, tm, sub):
        rows = slice(r0, r0 + sub)
        x = x_ref[rows, :]
        ms = jnp.mean(x * x, axis=-1, keepdims=True)
        h_ref[rows, :] = (x * lax.rsqrt(ms + RMS_EPS) * g_ref[...]).astype(BF16)
        for c in range(0, IN_COLS, IN_CHUNK):
            if OFF_AQ <= c < OFF_AQ + ATT_W:
                q0 = c - OFF_AQ
                acc = _dot(h_ref[rows, :], wq_ref[:, q0:q0 + IN_CHUNK]) * ATT_Q_SCALE
            else:
                acc = _dot(h_ref[rows, :], w_ref[:, c:c + IN_CHUNK])
            o_ref[rows, c:c + IN_CHUNK] = acc.astype(BF16)


def _inproj_call(x2d, gain, w_bf16, wq_bf16, tm):
    rows = x2d.shape[0]
    const = lambda shape: pl.BlockSpec(shape, lambda i: (0, 0), pipeline_mode=pl.Buffered(1))
    return pl.pallas_call(
        _inproj_kernel,
        out_shape=jax.ShapeDtypeStruct((rows, IN_COLS), BF16),
        grid=(rows // tm,),
        in_specs=[
            pl.BlockSpec((tm, D_MODEL), lambda i: (i, 0)),
            pl.BlockSpec((1, D_MODEL), lambda i: (0, 0)),
            const((D_MODEL, IN_COLS)), const((D_MODEL, ATT_W)),
        ],
        out_specs=pl.BlockSpec((tm, IN_COLS), lambda i: (i, 0)),
        scratch_shapes=[pltpu.VMEM((tm, D_MODEL), BF16)],
        compiler_params=pltpu.CompilerParams(
            dimension_semantics=("parallel",), vmem_limit_bytes=VMEM_LIMIT),
        name="inproj",
    )(x2d, gain, w_bf16, wq_bf16)


META_RK, META_RV, META_AK, META_AV = 0, 512, 1536, 1792
META_COLS = 2048


def _meta_kernel(x_ref, g_ref, wk_ref, wv_ref, wa_ref, o_ref):
    x = x_ref[...]
    ms = jnp.mean(x * x, axis=-1, keepdims=True)
    h = (x * lax.rsqrt(ms + RMS_EPS) * g_ref[...]).astype(BF16)
    o_ref[:, META_RK:META_RV] = _dot(h, wk_ref[...]).astype(BF16)
    o_ref[:, META_RV:META_AK] = _dot(h, wv_ref[...]).astype(BF16)
    o_ref[:, META_AK:META_COLS] = _dot(h, wa_ref[...]).astype(BF16)


def _meta_call(meta, gain, w_bf16):
    kv_w = ATT_KV_HEADS * ATT_DH
    assert OFF_AV == OFF_AK + kv_w
    piece = lambda width, off: pl.BlockSpec((D_MODEL, width), lambda i: (0, off // width))
    return pl.pallas_call(
        _meta_kernel,
        out_shape=jax.ShapeDtypeStruct((N_META, META_COLS), BF16),
        grid=(1,),
        in_specs=[
            pl.BlockSpec((N_META, D_MODEL), lambda i: (0, 0)),
            pl.BlockSpec((1, D_MODEL), lambda i: (0, 0)),
            piece(RET_HEADS * RET_DK, OFF_RK), piece(RET_W, OFF_RV), piece(2 * kv_w, OFF_AK),
        ],
        out_specs=pl.BlockSpec((N_META, META_COLS), lambda i: (0, 0)),
        name="metaproj",
    )(meta, gain, w_bf16, w_bf16, w_bf16)


RET_C = 256
RET_BATCH = 2


def _retention_kernel(lg_ref, q_ref, k_ref, v_ref, km_ref, vm_ref, o_ref,
                      tq_ref, tk_ref, td_ref, sel_ref, ts_ref, tm_ref, u_ref, s_ref, i_ref,
                      *, n_chunks):
    pair = pl.program_id(0)
    C = RET_C
    k_scale = RET_DK ** -0.5

    @pl.when(pl.program_id(1) == 0)
    def _build_tables():
        lgf = [lg_ref[0, 2 * pair + hl] for hl in range(2)]
        lgb = [lg_ref[1, 2 * pair + hl] for hl in range(2)]
        row = lax.broadcasted_iota(jnp.int32, (C, LANES), 0).astype(F32)
        first = lax.broadcasted_iota(jnp.int32, (C, LANES), 1) < RET_DK
        pos_t = lax.broadcasted_iota(jnp.int32, (LANES, C), 1).astype(F32)
        first_t = lax.broadcasted_iota(jnp.int32, (LANES, C), 0) < RET_DK
        diff = (lax.broadcasted_iota(jnp.int32, (C, C), 0)
                - lax.broadcasted_iota(jnp.int32, (C, C), 1)).astype(F32)
        srow = lax.broadcasted_iota(jnp.int32, (LANES, 2 * RET_DV), 0)
        scol = lax.broadcasted_iota(jnp.int32, (LANES, 2 * RET_DV), 1)
        mrow = lax.broadcasted_iota(jnp.int32, (N_META, LANES), 0).astype(F32)
        mfirst = lax.broadcasted_iota(jnp.int32, (N_META, LANES), 1) < RET_DK

        def both(sel, fn, lg):
            return jnp.where(sel, fn(lg[0]), fn(lg[1]))

        tq_ref[0] = both(first, lambda l: jnp.exp((row + 1.0) * l), lgf)
        tq_ref[1] = both(first, lambda l: jnp.exp((C - row) * l), lgb)
        tk_ref[0] = both(first_t, lambda l: jnp.exp((C - 1.0 - pos_t) * l), lgf) * k_scale
        tk_ref[1] = both(first_t, lambda l: jnp.exp(pos_t * l), lgb) * k_scale
        tm_ref[...] = both(mfirst, lambda l: jnp.exp((N_META - 1.0 - mrow) * l), lgf) * k_scale
        for hl in range(2):
            td_ref[hl] = k_scale * jnp.where(diff >= 0, jnp.exp(jnp.maximum(diff, 0.0) * lgf[hl]),
                                             jnp.exp(jnp.maximum(-diff, 0.0) * lgb[hl]))
            sel_ref[hl] = jnp.where(first == (hl == 0), 1.0, 0.0).astype(BF16)
        own = (srow < RET_DK) == (scol < RET_DV)
        zeros = jnp.zeros((LANES, 2 * RET_DV), F32)
        ts_ref[0] = jnp.where(own, both(srow < RET_DK, lambda l: jnp.exp(zeros + C * l), lgf), 0.0)
        ts_ref[1] = jnp.where(own, both(srow < RET_DK, lambda l: jnp.exp(zeros + C * l), lgb), 0.0)
        ts_ref[2] = jnp.where(own, 1.0, 0.0)

    def head(x, hl):
        return x[:, hl * RET_DV:(hl + 1) * RET_DV]

    own_mask = ts_ref[2]
    for bb in range(RET_BATCH):
        _retention_one(q_ref.at[bb], k_ref.at[bb], v_ref.at[bb], km_ref, vm_ref, o_ref.at[bb],
                       tq_ref, tk_ref, td_ref, sel_ref, ts_ref, tm_ref, u_ref.at[bb], s_ref.at[bb],
                       i_ref.at[bb], own_mask, head, n_chunks)


def _retention_one(q_ref, k_ref, v_ref, km_ref, vm_ref, o_ref, tq_ref, tk_ref, td_ref, sel_ref,
                   ts_ref, tm_ref, u_ref, s_ref, i_ref, own_mask, head, n_chunks):
    C = RET_C

    for c in range(n_chunks):
        rows = slice(c * C, (c + 1) * C)
        kc = k_ref[rows, :]
        vc = v_ref[rows, :]
        k_t = kc.astype(F32).T
        lhs = jnp.concatenate([k_t * tk_ref[0], k_t * tk_ref[1]], axis=0).astype(BF16)
        u = _dot(lhs, vc)
        u_ref[c, 0:LANES, :] = u[0:LANES] * own_mask
        u_ref[c, LANES:2 * LANES, :] = u[LANES:2 * LANES] * own_mask
        q16 = q_ref[rows, :]
        for hl in range(2):
            s = _dot_nt(q16 * sel_ref[hl], kc) * td_ref[hl]
            i_ref[c, :, hl * RET_DV:(hl + 1) * RET_DV] = _dot(s.astype(BF16), head(vc, hl))

    sf = _dot_tn((km_ref[...].astype(F32) * tm_ref[...]).astype(BF16), vm_ref[...]) * own_mask
    for c in range(n_chunks):
        s_ref[c, 0:LANES, :] = sf.astype(BF16)
        if c + 1 < n_chunks:
            sf = sf * ts_ref[0] + u_ref[c, 0:LANES, :]
    sb = jnp.zeros((LANES, 2 * RET_DV), F32)
    for c in reversed(range(n_chunks)):
        s_ref[c, LANES:2 * LANES, :] = sb.astype(BF16)
        if c > 0:
            sb = sb * ts_ref[1] + u_ref[c, LANES:2 * LANES, :]

    for c in range(n_chunks):
        rows = slice(c * C, (c + 1) * C)
        qc = q_ref[rows, :].astype(F32)
        q_cross = jnp.concatenate([(qc * tq_ref[0]).astype(BF16),
                                   (qc * tq_ref[1]).astype(BF16)], axis=1)
        o_ref[rows, :] = (i_ref[c] + _dot(q_cross, s_ref[c])).astype(BF16)


def _retention_call(log_gammas, proj, proj_meta, batch, seq):
    assert seq % RET_C == 0 and batch % RET_BATCH == 0
    n_chunks = seq // RET_C
    n_pairs = RET_HEADS // 2
    kernel = functools.partial(_retention_kernel, n_chunks=n_chunks)
    qb, kb = OFF_RQ // LANES, OFF_RK // LANES
    vb = OFF_RV // (2 * RET_DV)
    return pl.pallas_call(
        kernel,
        out_shape=jax.ShapeDtypeStruct((batch, seq, RET_W), BF16),
        grid=(n_pairs, batch // RET_BATCH),
        in_specs=[
            pl.BlockSpec(memory_space=pltpu.SMEM),
            pl.BlockSpec((RET_BATCH, seq, LANES), lambda p, b: (b, 0, qb + p)),
            pl.BlockSpec((RET_BATCH, seq, LANES), lambda p, b: (b, 0, kb + p)),
            pl.BlockSpec((RET_BATCH, seq, 2 * RET_DV), lambda p, b: (b, 0, vb + p)),
            pl.BlockSpec((N_META, LANES), lambda p, b: (0, META_RK // LANES + p)),
            pl.BlockSpec((N_META, 2 * RET_DV), lambda p, b: (0, META_RV // (2 * RET_DV) + p)),
        ],
        out_specs=pl.BlockSpec((RET_BATCH, seq, 2 * RET_DV), lambda p, b: (b, 0, p)),
        scratch_shapes=[
            pltpu.VMEM((2, RET_C, LANES), F32),
            pltpu.VMEM((2, LANES, RET_C), F32),
            pltpu.VMEM((2, RET_C, RET_C), F32),
            pltpu.VMEM((2, RET_C, LANES), BF16),
            pltpu.VMEM((3, LANES, 2 * RET_DV), F32),
            pltpu.VMEM((N_META, LANES), F32),
            pltpu.VMEM((RET_BATCH, n_chunks, 2 * LANES, 2 * RET_DV), F32),
            pltpu.VMEM((RET_BATCH, n_chunks, 2 * LANES, 2 * RET_DV), BF16),
            pltpu.VMEM((RET_BATCH, n_chunks, RET_C, 2 * RET_DV), F32),
        ],
        compiler_params=pltpu.CompilerParams(
            dimension_semantics=("arbitrary", "arbitrary"), vmem_limit_bytes=VMEM_LIMIT),
        name="retention",
    )(log_gammas, proj, proj, proj, proj_meta, proj_meta)


def _pair_head_cols(w):
    w5 = w.reshape(w.shape[0], ATT_KV_HEADS // 2, 2, ATT_GROUP, ATT_DH)
    return w5.transpose(0, 1, 3, 2, 4).reshape(w.shape)


QB = 64
BAND = QB + 2 * WINDOW
META_PAD = 64
N_KEYS = BAND + META_PAD
ATT_ROWS = 1024
N_OFFSETS = 2 * WINDOW // QB + 1
SINK_COL = BAND + N_META
HB_UNROLL = 8


def _attention_kernel(sink_ref, q_ref, k_ref, v_ref, km_ref, vm_ref, *rest, seq, n_cast):
    cast_src, o_ref = rest[:n_cast], rest[n_cast]
    cast_dst, bias_ref = rest[n_cast + 1:2 * n_cast + 1], rest[2 * n_cast + 1]
    for src, dst in zip(cast_src, cast_dst):
        dst[...] = src[...].astype(BF16)
    _attention_body(sink_ref, q_ref, k_ref, v_ref, km_ref, vm_ref, o_ref, bias_ref, seq=seq)


def _attention_body(sink_ref, q_ref, k_ref, v_ref, km_ref, vm_ref, o_ref, bias_ref, *, seq):
    n = pl.program_id(1)

    @pl.when((pl.program_id(0) == 0) & (n == 0))
    def _build_bias():
        r = lax.broadcasted_iota(jnp.int32, (QB, N_KEYS), 0)
        j = lax.broadcasted_iota(jnp.int32, (QB, N_KEYS), 1)
        for t in range(N_OFFSETS):
            dist = jnp.abs(j - t * QB - r)
            band_ok = (j < BAND) & (dist <= WINDOW)
            ndist = dist.astype(F32) * (-LOG2E)
            for head in range(ATT_Q_HEADS):
                slope = 2.0 ** (-8.0 * (head + 1) / ATT_Q_HEADS)
                sink = sink_ref[head] * LOG2E
                rest = jnp.where((j >= BAND) & (j < SINK_COL), 0.0,
                                 jnp.where(j == SINK_COL, sink, NEG_INF))
                bias_ref[t, head] = jnp.where(band_ok, slope * ndist, rest)

    low_k = lax.broadcasted_iota(jnp.int32, (N_KEYS, LANES), 1) < ATT_DH

    def block_diag(x):
        zero = jnp.zeros_like(x)
        return jnp.concatenate([jnp.where(low_k, x, zero), jnp.where(low_k, zero, x)], axis=0)

    ones_bd = jnp.concatenate([jnp.where(low_k, 1.0, 0.0), jnp.where(low_k, 0.0, 1.0)],
                              axis=0).astype(BF16)
    meta_fill = jnp.zeros((META_PAD - N_META, LANES), BF16)

    def body(it, carry):
        for u in range(HB_UNROLL):
            hb = it * HB_UNROLL + u
            row0 = n * ATT_ROWS + hb * QB
            start = pl.multiple_of(jnp.clip(row0 - WINDOW, 0, seq - BAND), QB)
            t = (row0 - start) // QB
            rows = pl.ds(pl.multiple_of(hb * QB, QB), QB)
            for jp in range(ATT_KV_HEADS // 2):
                ksl = slice(jp * LANES, (jp + 1) * LANES)
                k_bd = block_diag(jnp.concatenate(
                    [k_ref[pl.ds(start, BAND), ksl], km_ref[:, ksl], meta_fill], axis=0))
                v_bd = jnp.concatenate([block_diag(jnp.concatenate(
                    [v_ref[pl.ds(start, BAND), ksl], vm_ref[:, ksl], meta_fill], axis=0)),
                    ones_bd], axis=1)
                groups = [slice((jp * ATT_GROUP + g) * LANES, (jp * ATT_GROUP + g + 1) * LANES)
                          for g in range(ATT_GROUP)]
                q_stack = jnp.concatenate([q_ref[rows, grp] for grp in groups], axis=0)
                s = _dot_nt(q_stack, k_bd)
                probs = []
                for g in range(ATT_GROUP):
                    p_halves = []
                    for half in range(2):
                        head = (2 * jp + half) * ATT_GROUP + g
                        sg = (s[g * QB:(g + 1) * QB, half * N_KEYS:(half + 1) * N_KEYS]
                              + bias_ref[t, head])
                        p = jnp.exp2(sg - jnp.max(sg, axis=-1, keepdims=True))
                        p_halves.append(p.astype(BF16))
                    probs.append(jnp.concatenate(p_halves, axis=1))
                o = _dot(jnp.concatenate(probs, axis=0), v_bd)
                out = o[:, 0:LANES] / o[:, LANES:2 * LANES]
                for g in range(ATT_GROUP):
                    o_ref[rows, groups[g]] = out[g * QB:(g + 1) * QB].astype(BF16)
        return carry

    lax.fori_loop(0, ATT_ROWS // QB // HB_UNROLL, body, 0)


def _paired_src_block(i, blocks_per_group):
    grp, sub = i // blocks_per_group, i % blocks_per_group
    j, g, half = grp // (2 * ATT_GROUP), (grp // 2) % ATT_GROUP, grp % 2
    return ((j * 2 + half) * ATT_GROUP + g) * blocks_per_group + sub


def _attention_call(sink, proj, proj_meta, w_casts, batch, seq):
    assert seq % ATT_ROWS == 0 and seq >= BAND
    n_blk = seq // ATT_ROWS
    steps = batch * n_blk
    kernel = functools.partial(_attention_kernel, seq=seq, n_cast=len(w_casts))
    kv_w = ATT_KV_HEADS * ATT_DH
    step = lambda b, n: b * n_blk + n
    cast_in, cast_out, cast_shapes = [], [], []
    for idx, w in enumerate(w_casts):
        rows, cols = w.shape
        reps = 1
        while rows % (steps // reps) or (rows // (steps // reps)) % 16:
            reps *= 2
        blk = rows // (steps // reps)
        dst_map = lambda b, n, reps=reps: (step(b, n) // reps, 0)
        if idx == 1:
            assert reps == 1 and ATT_DH % blk == 0
            src_map = lambda b, n, bpg=ATT_DH // blk: (_paired_src_block(step(b, n), bpg), 0)
        else:
            src_map = dst_map
        cast_in.append(pl.BlockSpec((blk, cols), src_map))
        cast_out.append(pl.BlockSpec((blk, cols), dst_map))
        cast_shapes.append(jax.ShapeDtypeStruct((rows, cols), BF16))
    return pl.pallas_call(
        kernel,
        out_shape=[jax.ShapeDtypeStruct((batch, seq, ATT_W), BF16)] + cast_shapes,
        grid=(batch, n_blk),
        in_specs=[
            pl.BlockSpec(memory_space=pltpu.SMEM),
            pl.BlockSpec((None, ATT_ROWS, ATT_W), lambda b, n: (b, n, OFF_AQ // ATT_W)),
            pl.BlockSpec((None, seq, kv_w), lambda b, n: (b, 0, OFF_AK // kv_w)),
            pl.BlockSpec((None, seq, kv_w), lambda b, n: (b, 0, OFF_AV // kv_w)),
            pl.BlockSpec((N_META, kv_w), lambda b, n: (0, META_AK // kv_w)),
            pl.BlockSpec((N_META, kv_w), lambda b, n: (0, META_AV // kv_w)),
        ] + cast_in,
        out_specs=[pl.BlockSpec((None, ATT_ROWS, ATT_W), lambda b, n: (b, n, 0))] + cast_out,
        scratch_shapes=[pltpu.VMEM((N_OFFSETS, ATT_Q_HEADS, QB, N_KEYS), F32)],
        compiler_params=pltpu.CompilerParams(
            dimension_semantics=("arbitrary", "arbitrary"), vmem_limit_bytes=VMEM_LIMIT),
        name="attention",
    )(sink, proj, proj, proj, proj_meta, proj_meta, *w_casts)


FF_CHUNK = 256
GATE_HALF = D_MODEL // 2


def _mixffn_kernel(x_ref, ret_ref, rg_ref, gain_ref, att_ref, ga0_ref, ga1_ref, gb0_ref, gb1_ref,
                   wr_ref, wa_ref, wo_ref, nf_ref, wg_ref, wu_ref, wd_ref, nfin_ref, o_ref,
                   r_ref, h_ref, act_ref):
    gb = jnp.concatenate([gb0_ref[...], gb1_ref[...]], axis=1).astype(F32)
    att_part = _sigmoid(gb) * _dot(att_ref[...], wa_ref[...])
    for h in range(RET_HEADS):
        sl = slice(h * RET_DV, (h + 1) * RET_DV)
        o = ret_ref[:, sl].astype(F32)
        d = o - jnp.mean(o, axis=-1, keepdims=True)
        var = jnp.mean(d * d, axis=-1, keepdims=True)
        g = rg_ref[:, sl].astype(F32)
        r_ref[:, sl] = (g * _sigmoid(g) * (d * lax.rsqrt(var + GN_EPS) * gain_ref[:, sl])).astype(BF16)
    ga = jnp.concatenate([ga0_ref[...], ga1_ref[...]], axis=1).astype(F32)
    merged = _sigmoid(ga) * _dot(r_ref[...], wr_ref[...]) + att_part
    x = x_ref[...] + _dot(merged.astype(BF16), wo_ref[...])
    ms = jnp.mean(x * x, axis=-1, keepdims=True)
    h_ref[...] = (x * lax.rsqrt(ms + RMS_EPS) * nf_ref[...]).astype(BF16)
    for c in range(0, D_FF, FF_CHUNK):
        a = _dot(h_ref[...], wg_ref[:, c:c + FF_CHUNK])
        b = _dot(h_ref[...], wu_ref[:, c:c + FF_CHUNK])
        act_ref[:, c:c + FF_CHUNK] = (a * _sigmoid(a) * b).astype(BF16)
    y = x + _dot(act_ref[...], wd_ref[...])
    ms = jnp.mean(y * y, axis=-1, keepdims=True)
    o_ref[...] = y * lax.rsqrt(ms + RMS_EPS) * nfin_ref[...]


def _mixffn_call(x2d, ret, gn_gain, att, proj, wr, wa, wo, norm_ffn, wgu, wd, norm_final, tm):
    rows = x2d.shape[0]
    const = lambda shape, col=0: pl.BlockSpec(shape, lambda i: (0, col),
                                              pipeline_mode=pl.Buffered(1))
    row_tile = lambda col: pl.BlockSpec((tm, D_MODEL), lambda i: (i, col))
    half_tile = lambda col: pl.BlockSpec((tm, GATE_HALF), lambda i: (i, col))
    return pl.pallas_call(
        _mixffn_kernel,
        out_shape=jax.ShapeDtypeStruct((rows, D_MODEL), F32),
        grid=(rows // tm,),
        in_specs=[
            row_tile(0), row_tile(0), row_tile(OFF_RG // RET_W), const((1, RET_W)), row_tile(0),
            half_tile(OFF_GA // GATE_HALF), half_tile(OFF_GA // GATE_HALF + 1),
            half_tile(OFF_GB // GATE_HALF), half_tile(OFF_GB // GATE_HALF + 1),
            const((RET_W, D_MODEL)), const((ATT_W, D_MODEL)), const((D_MODEL, D_MODEL)),
            const((1, D_MODEL)),
            const((D_MODEL, D_FF), 0), const((D_MODEL, D_FF), 1), const((D_FF, D_MODEL)),
            const((1, D_MODEL)),
        ],
        out_specs=row_tile(0),
        scratch_shapes=[pltpu.VMEM((tm, RET_W), BF16), pltpu.VMEM((tm, D_MODEL), BF16),
                        pltpu.VMEM((tm, D_FF), BF16)],
        compiler_params=pltpu.CompilerParams(
            dimension_semantics=("parallel",), vmem_limit_bytes=VMEM_LIMIT),
        name="mixffn",
    )(x2d, ret, proj, gn_gain, att, proj, proj, proj, proj, wr, wa, wo, norm_ffn, wgu, wgu, wd,
      norm_final)


def kernel(x, meta_tokens, w_in, ret_decay_logit_fwd, ret_decay_logit_bwd, ret_gn_gain, attn_sink,
           w_branch_ret, w_branch_att, w_out, norm_mix, norm_ffn, w_gate_up, w_down, norm_final):
    batch, seq, d = x.shape
    assert d == D_MODEL and w_in.shape[0] == 1

    w_in0 = w_in[0]
    w_in_b = w_in0.astype(BF16)
    w_q_b = _pair_head_cols(w_in0[:, OFF_AQ:OFF_AK]).astype(BF16)
    nmix = norm_mix[0].reshape(1, D_MODEL)
    nffn = norm_ffn[0].reshape(1, D_MODEL)
    nfin = norm_final.reshape(1, D_MODEL)

    x2d = x.reshape(batch * seq, D_MODEL)
    proj = _inproj_call(x2d, nmix, w_in_b, w_q_b, 1024)
    proj_meta = _meta_call(meta_tokens.astype(x.dtype), nmix, w_in_b)
    proj3 = proj.reshape(batch, seq, IN_COLS)

    log_gammas = jnp.stack([jax.nn.log_sigmoid(ret_decay_logit_fwd[0].astype(F32)),
                            jax.nn.log_sigmoid(ret_decay_logit_bwd[0].astype(F32))])
    ret = _retention_call(log_gammas, proj3, proj_meta, batch, seq)

    att, wr, wa, wo, wgu, wd = _attention_call(
        attn_sink[0].astype(F32), proj3, proj_meta,
        (w_branch_ret[0], w_branch_att[0], w_out[0], w_gate_up[0], w_down[0]), batch, seq)

    out = _mixffn_call(x2d, ret.reshape(batch * seq, RET_W), ret_gn_gain[0].reshape(1, RET_W),
                       att.reshape(batch * seq, ATT_W), proj,
                       wr, wa, wo, nffn, wgu, wd, nfin, 512)
    return out.reshape(batch, seq, D_MODEL)
```

```python
import functools

import numpy as np
import jax
import jax.numpy as jnp
from jax import lax
from jax.experimental import pallas as pl
from jax.experimental.pallas import tpu as pltpu

D_MODEL = 1024
N_META = 16
RET_HEADS = 8
RET_DK = 64
RET_DV = 128
RET_W = RET_HEADS * RET_DV
ATT_Q_HEADS = 16
ATT_KV_HEADS = 4
ATT_GROUP = ATT_Q_HEADS // ATT_KV_HEADS
ATT_DH = 64
ATT_W = ATT_Q_HEADS * ATT_DH
WINDOW = 128
D_FF = 2816
IN_SIZES = (512, 512, 1024, 1024, 1024, 256, 256, 1024, 1024)
IN_COLS = sum(IN_SIZES)
OFF_RQ, OFF_RK, OFF_RV, OFF_RG, OFF_AQ, OFF_AK, OFF_AV, OFF_GA, OFF_GB = (
    int(c) for c in np.cumsum((0,) + IN_SIZES[:-1]))
RMS_EPS = 1e-6
GN_EPS = 1e-5
NEG_INF = -1e30
LOG2E = 1.4426950408889634
ATT_Q_SCALE = ATT_DH ** -0.5 * LOG2E

LANES = 128
VMEM_LIMIT = 56 * 1024 * 1024

F32 = jnp.float32
BF16 = jnp.bfloat16


def _sigmoid(x):
    return 1.0 / (1.0 + jnp.exp(-x))


def _dot(a, b):
    return jnp.dot(a, b, preferred_element_type=F32)


def _dot_nt(a, b):
    return lax.dot_general(a, b, (((1,), (1,)), ((), ())), preferred_element_type=F32)


def _dot_tn(a, b):
    return lax.dot_general(a, b, (((0,), (0,)), ((), ())), preferred_element_type=F32)


IN_CHUNK = 512
IN_SUB = 512


def _inproj_kernel(x_ref, g_ref, w_ref, wq_ref, o_ref, h_ref):
    tm = x_ref.shape[0]
    sub = min(IN_SUB, tm)
    for r0 in range(0, tm, sub):
        rows = slice(r0, r0 + sub)
        x = x_ref[rows, :]
        ms = jnp.mean(x * x, axis=-1, keepdims=True)
        h_ref[rows, :] = (x * lax.rsqrt(ms + RMS_EPS) * g_ref[...]).astype(BF16)
        for c in range(0, IN_COLS, IN_CHUNK):
            if OFF_AQ <= c < OFF_AQ + ATT_W:
                q0 = c - OFF_AQ
                acc = _dot(h_ref[rows, :], wq_ref[:, q0:q0 + IN_CHUNK]) * ATT_Q_SCALE
            else:
                acc = _dot(h_ref[rows, :], w_ref[:, c:c + IN_CHUNK])
            o_ref[rows, c:c + IN_CHUNK] = acc.astype(BF16)


def _inproj_call(x2d, gain, w_bf16, wq_bf16, tm):
    rows = x2d.shape[0]
    const = lambda shape: pl.BlockSpec(shape, lambda i: (0, 0), pipeline_mode=pl.Buffered(1))
    return pl.pallas_call(
        _inproj_kernel,
        out_shape=jax.ShapeDtypeStruct((rows, IN_COLS), BF16),
        grid=(rows // tm,),
        in_specs=[
            pl.BlockSpec((tm, D_MODEL), lambda i: (i, 0)),
            pl.BlockSpec((1, D_MODEL), lambda i: (0, 0)),
            const((D_MODEL, IN_COLS)), const((D_MODEL, ATT_W)),
        ],
        out_specs=pl.BlockSpec((tm, IN_COLS), lambda i: (i, 0)),
        scratch_shapes=[pltpu.VMEM((tm, D_MODEL), BF16)],
        compiler_params=pltpu.CompilerParams(
            dimension_semantics=("parallel",), vmem_limit_bytes=VMEM_LIMIT),
        name="inproj",
    )(x2d, gain, w_bf16, wq_bf16)


META_RK, META_RV, META_AK, META_AV = 0, 512, 1536, 1792
META_COLS = 2048


def _meta_kernel(x_ref, g_ref, wk_ref, wv_ref, wa_ref, o_ref):
    x = x_ref[...]
    ms = jnp.mean(x * x, axis=-1, keepdims=True)
    h = (x * lax.rsqrt(ms + RMS_EPS) * g_ref[...]).astype(BF16)
    o_ref[:, META_RK:META_RV] = _dot(h, wk_ref[...]).astype(BF16)
    o_ref[:, META_RV:META_AK] = _dot(h, wv_ref[...]).astype(BF16)
    o_ref[:, META_AK:META_COLS] = _dot(h, wa_ref[...]).astype(BF16)


def _meta_call(meta, gain, w_bf16):
    kv_w = ATT_KV_HEADS * ATT_DH
    assert OFF_AV == OFF_AK + kv_w
    piece = lambda width, off: pl.BlockSpec((D_MODEL, width), lambda i: (0, off // width))
    return pl.pallas_call(
        _meta_kernel,
        out_shape=jax.ShapeDtypeStruct((N_META, META_COLS), BF16),
        grid=(1,),
        in_specs=[
            pl.BlockSpec((N_META, D_MODEL), lambda i: (0, 0)),
            pl.BlockSpec((1, D_MODEL), lambda i: (0, 0)),
            piece(RET_HEADS * RET_DK, OFF_RK), piece(RET_W, OFF_RV), piece(2 * kv_w, OFF_AK),
        ],
        out_specs=pl.BlockSpec((N_META, META_COLS), lambda i: (0, 0)),
        name="metaproj",
    )(meta, gain, w_bf16, w_bf16, w_bf16)


RET_C = 256
RET_BATCH = 4


def _retention_kernel(lg_ref, q_ref, k_ref, v_ref, km_ref, vm_ref, o_ref,
                      tq_ref, tk_ref, td_ref, sel_ref, ts_ref, tm_ref, u_ref, s_ref, i_ref,
                      *, n_chunks):
    pair = pl.program_id(0)
    C = RET_C
    k_scale = RET_DK ** -0.5

    @pl.when(pl.program_id(1) == 0)
    def _build_tables():
        lgf = [lg_ref[0, 2 * pair + hl] for hl in range(2)]
        lgb = [lg_ref[1, 2 * pair + hl] for hl in range(2)]
        row = lax.broadcasted_iota(jnp.int32, (C, LANES), 0).astype(F32)
        first = lax.broadcasted_iota(jnp.int32, (C, LANES), 1) < RET_DK
        pos_t = lax.broadcasted_iota(jnp.int32, (LANES, C), 1).astype(F32)
        first_t = lax.broadcasted_iota(jnp.int32, (LANES, C), 0) < RET_DK
        diff = (lax.broadcasted_iota(jnp.int32, (C, C), 0)
                - lax.broadcasted_iota(jnp.int32, (C, C), 1)).astype(F32)
        srow = lax.broadcasted_iota(jnp.int32, (LANES, 2 * RET_DV), 0)
        scol = lax.broadcasted_iota(jnp.int32, (LANES, 2 * RET_DV), 1)
        mrow = lax.broadcasted_iota(jnp.int32, (N_META, LANES), 0).astype(F32)
        mfirst = lax.broadcasted_iota(jnp.int32, (N_META, LANES), 1) < RET_DK

        def both(sel, fn, lg):
            return jnp.where(sel, fn(lg[0]), fn(lg[1]))

        tq_ref[0] = both(first, lambda l: jnp.exp((row + 1.0) * l), lgf)
        tq_ref[1] = both(first, lambda l: jnp.exp((C - row) * l), lgb)
        tk_ref[0] = both(first_t, lambda l: jnp.exp((C - 1.0 - pos_t) * l), lgf) * k_scale
        tk_ref[1] = both(first_t, lambda l: jnp.exp(pos_t * l), lgb) * k_scale
        tm_ref[...] = both(mfirst, lambda l: jnp.exp((N_META - 1.0 - mrow) * l), lgf) * k_scale
        for hl in range(2):
            td_ref[hl] = k_scale * jnp.where(diff >= 0, jnp.exp(jnp.maximum(diff, 0.0) * lgf[hl]),
                                             jnp.exp(jnp.maximum(-diff, 0.0) * lgb[hl]))
            sel_ref[hl] = jnp.where(first == (hl == 0), 1.0, 0.0).astype(BF16)
        own = (srow < RET_DK) == (scol < RET_DV)
        zeros = jnp.zeros((LANES, 2 * RET_DV), F32)
        ts_ref[0] = jnp.where(own, both(srow < RET_DK, lambda l: jnp.exp(zeros + C * l), lgf), 0.0)
        ts_ref[1] = jnp.where(own, both(srow < RET_DK, lambda l: jnp.exp(zeros + C * l), lgb), 0.0)
        ts_ref[2] = jnp.where(own, 1.0, 0.0)

    def head(x, hl):
        return x[:, hl * RET_DV:(hl + 1) * RET_DV]

    own_mask = ts_ref[2]
    for bb in range(RET_BATCH):
        _retention_one(q_ref.at[bb], k_ref.at[bb], v_ref.at[bb], km_ref, vm_ref, o_ref.at[bb],
                       tq_ref, tk_ref, td_ref, sel_ref, ts_ref, tm_ref, u_ref.at[bb], s_ref.at[bb],
                       i_ref.at[bb], own_mask, head, n_chunks)


def _retention_one(q_ref, k_ref, v_ref, km_ref, vm_ref, o_ref, tq_ref, tk_ref, td_ref, sel_ref,
                   ts_ref, tm_ref, u_ref, s_ref, i_ref, own_mask, head, n_chunks):
    C = RET_C

    for c in range(n_chunks):
        rows = slice(c * C, (c + 1) * C)
        kc = k_ref[rows, :]
        vc = v_ref[rows, :]
        k_t = kc.astype(F32).T
        lhs = jnp.concatenate([k_t * tk_ref[0], k_t * tk_ref[1]], axis=0).astype(BF16)
        u = _dot(lhs, vc)
        u_ref[c, 0:LANES, :] = u[0:LANES] * own_mask
        u_ref[c, LANES:2 * LANES, :] = u[LANES:2 * LANES] * own_mask
        q16 = q_ref[rows, :]
        for hl in range(2):
            s = _dot_nt(q16 * sel_ref[hl], kc) * td_ref[hl]
            i_ref[c, :, hl * RET_DV:(hl + 1) * RET_DV] = _dot(s.astype(BF16), head(vc, hl))

    sf = _dot_tn((km_ref[...].astype(F32) * tm_ref[...]).astype(BF16), vm_ref[...]) * own_mask
    for c in range(n_chunks):
        s_ref[c, 0:LANES, :] = sf.astype(BF16)
        if c + 1 < n_chunks:
            sf = sf * ts_ref[0] + u_ref[c, 0:LANES, :]
    sb = jnp.zeros((LANES, 2 * RET_DV), F32)
    for c in reversed(range(n_chunks)):
        s_ref[c, LANES:2 * LANES, :] = sb.astype(BF16)
        if c > 0:
            sb = sb * ts_ref[1] + u_ref[c, LANES:2 * LANES, :]

    for c in range(n_chunks):
        rows = slice(c * C, (c + 1) * C)
        qc = q_ref[rows, :].astype(F32)
        q_cross = jnp.concatenate([(qc * tq_ref[0]).astype(BF16),
                                   (qc * tq_ref[1]).astype(BF16)], axis=1)
        o_ref[rows, :] = (i_ref[c] + _dot(q_cross, s_ref[c])).astype(BF16)


def _retention_call(log_gammas, proj, proj_meta, batch, seq):
    assert seq % RET_C == 0 and batch % RET_BATCH == 0
    n_chunks = seq // RET_C
    n_pairs = RET_HEADS // 2
    kernel = functools.partial(_retention_kernel, n_chunks=n_chunks)
    qb, kb = OFF_RQ // LANES, OFF_RK // LANES
    vb = OFF_RV // (2 * RET_DV)
    return pl.pallas_call(
        kernel,
        out_shape=jax.ShapeDtypeStruct((batch, seq, RET_W), BF16),
        grid=(n_pairs, batch // RET_BATCH),
        in_specs=[
            pl.BlockSpec(memory_space=pltpu.SMEM),
            pl.BlockSpec((RET_BATCH, seq, LANES), lambda p, b: (b, 0, qb + p)),
            pl.BlockSpec((RET_BATCH, seq, LANES), lambda p, b: (b, 0, kb + p)),
            pl.BlockSpec((RET_BATCH, seq, 2 * RET_DV), lambda p, b: (b, 0, vb + p)),
            pl.BlockSpec((N_META, LANES), lambda p, b: (0, META_RK // LANES + p)),
            pl.BlockSpec((N_META, 2 * RET_DV), lambda p, b: (0, META_RV // (2 * RET_DV) + p)),
        ],
        out_specs=pl.BlockSpec((RET_BATCH, seq, 2 * RET_DV), lambda p, b: (b, 0, p)),
        scratch_shapes=[
            pltpu.VMEM((2, RET_C, LANES), F32),
            pltpu.VMEM((2, LANES, RET_C), F32),
            pltpu.VMEM((2, RET_C, RET_C), F32),
            pltpu.VMEM((2, RET_C, LANES), BF16),
            pltpu.VMEM((3, LANES, 2 * RET_DV), F32),
            pltpu.VMEM((N_META, LANES), F32),
            pltpu.VMEM((RET_BATCH, n_chunks, 2 * LANES, 2 * RET_DV), F32),
            pltpu.VMEM((RET_BATCH, n_chunks, 2 * LANES, 2 * RET_DV), BF16),
            pltpu.VMEM((RET_BATCH, n_chunks, RET_C, 2 * RET_DV), F32),
        ],
        compiler_params=pltpu.CompilerParams(
            dimension_semantics=("arbitrary", "arbitrary"), vmem_limit_bytes=VMEM_LIMIT),
        name="retention",
    )(log_gammas, proj, proj, proj, proj_meta, proj_meta)


def _pair_head_cols(w):
    w5 = w.reshape(w.shape[0], ATT_KV_HEADS // 2, 2, ATT_GROUP, ATT_DH)
    return w5.transpose(0, 1, 3, 2, 4).reshape(w.shape)


QB = 64
BAND = QB + 2 * WINDOW
META_PAD = 64
N_KEYS = BAND + META_PAD
ATT_ROWS = 2048
N_OFFSETS = 2 * WINDOW // QB + 1
SINK_COL = BAND + N_META
HB_UNROLL = 16


def _attention_kernel(sink_ref, q_ref, k_ref, v_ref, km_ref, vm_ref, *rest, seq, n_cast):
    cast_src, o_ref = rest[:n_cast], rest[n_cast]
    cast_dst, bias_ref = rest[n_cast + 1:2 * n_cast + 1], rest[2 * n_cast + 1]
    for src, dst in zip(cast_src, cast_dst):
        dst[...] = src[...].astype(BF16)
    _attention_body(sink_ref, q_ref, k_ref, v_ref, km_ref, vm_ref, o_ref, bias_ref, seq=seq)


def _attention_body(sink_ref, q_ref, k_ref, v_ref, km_ref, vm_ref, o_ref, bias_ref, *, seq):
    n = pl.program_id(1)

    @pl.when((pl.program_id(0) == 0) & (n == 0))
    def _build_bias():
        r = lax.broadcasted_iota(jnp.int32, (QB, N_KEYS), 0)
        j = lax.broadcasted_iota(jnp.int32, (QB, N_KEYS), 1)
        for t in range(N_OFFSETS):
            dist = jnp.abs(j - t * QB - r)
            band_ok = (j < BAND) & (dist <= WINDOW)
            ndist = dist.astype(F32) * (-LOG2E)
            for head in range(ATT_Q_HEADS):
                slope = 2.0 ** (-8.0 * (head + 1) / ATT_Q_HEADS)
                sink = sink_ref[head] * LOG2E
                rest = jnp.where((j >= BAND) & (j < SINK_COL), 0.0,
                                 jnp.where(j == SINK_COL, sink, NEG_INF))
                bias_ref[t, head] = jnp.where(band_ok, slope * ndist, rest)

    low_k = lax.broadcasted_iota(jnp.int32, (N_KEYS, LANES), 1) < ATT_DH

    def block_diag(x):
        zero = jnp.zeros_like(x)
        return jnp.concatenate([jnp.where(low_k, x, zero), jnp.where(low_k, zero, x)], axis=0)

    ones_bd = jnp.concatenate([jnp.where(low_k, 1.0, 0.0), jnp.where(low_k, 0.0, 1.0)],
                              axis=0).astype(BF16)
    meta_fill = jnp.zeros((META_PAD - N_META, LANES), BF16)

    def body(it, carry):
        for u in range(HB_UNROLL):
            hb = it * HB_UNROLL + u
            row0 = n * ATT_ROWS + hb * QB
            start = pl.multiple_of(jnp.clip(row0 - WINDOW, 0, seq - BAND), QB)
            t = (row0 - start) // QB
            rows = pl.ds(pl.multiple_of(hb * QB, QB), QB)
            for jp in range(ATT_KV_HEADS // 2):
                ksl = slice(jp * LANES, (jp + 1) * LANES)
                k_bd = block_diag(jnp.concatenate(
                    [k_ref[pl.ds(start, BAND), ksl], km_ref[:, ksl], meta_fill], axis=0))
                v_bd = jnp.concatenate([block_diag(jnp.concatenate(
                    [v_ref[pl.ds(start, BAND), ksl], vm_ref[:, ksl], meta_fill], axis=0)),
                    ones_bd], axis=1)
                groups = [slice((jp * ATT_GROUP + g) * LANES, (jp * ATT_GROUP + g + 1) * LANES)
                          for g in range(ATT_GROUP)]
                q_stack = jnp.concatenate([q_ref[rows, grp] for grp in groups], axis=0)
                s = _dot_nt(q_stack, k_bd)
                probs = []
                for g in range(ATT_GROUP):
                    p_halves = []
                    for half in range(2):
                        head = (2 * jp + half) * ATT_GROUP + g
                        sg = (s[g * QB:(g + 1) * QB, half * N_KEYS:(half + 1) * N_KEYS]
                              + bias_ref[t, head])
                        p = jnp.exp2(sg - jnp.max(sg, axis=-1, keepdims=True))
                        p_halves.append(p.astype(BF16))
                    probs.append(jnp.concatenate(p_halves, axis=1))
                o = _dot(jnp.concatenate(probs, axis=0), v_bd)
                out = o[:, 0:LANES] / o[:, LANES:2 * LANES]
                for g in range(ATT_GROUP):
                    o_ref[rows, groups[g]] = out[g * QB:(g + 1) * QB].astype(BF16)
        return carry

    lax.fori_loop(0, ATT_ROWS // QB // HB_UNROLL, body, 0)


def _paired_src_block(i, blocks_per_group):
    grp, sub = i // blocks_per_group, i % blocks_per_group
    j, g, half = grp // (2 * ATT_GROUP), (grp // 2) % ATT_GROUP, grp % 2
    return ((j * 2 + half) * ATT_GROUP + g) * blocks_per_group + sub


def _attention_call(sink, proj, proj_meta, w_casts, batch, seq):
    assert seq % ATT_ROWS == 0 and seq >= BAND
    n_blk = seq // ATT_ROWS
    steps = batch * n_blk
    kernel = functools.partial(_attention_kernel, seq=seq, n_cast=len(w_casts))
    kv_w = ATT_KV_HEADS * ATT_DH
    step = lambda b, n: b * n_blk + n
    cast_in, cast_out, cast_shapes = [], [], []
    for idx, w in enumerate(w_casts):
        rows, cols = w.shape
        reps = 1
        while rows % (steps // reps) or (rows // (steps // reps)) % 16:
            reps *= 2
        blk = rows // (steps // reps)
        dst_map = lambda b, n, reps=reps: (step(b, n) // reps, 0)
        if idx == 1:
            assert reps == 1 and ATT_DH % blk == 0
            src_map = lambda b, n, bpg=ATT_DH // blk: (_paired_src_block(step(b, n), bpg), 0)
        else:
            src_map = dst_map
        cast_in.append(pl.BlockSpec((blk, cols), src_map))
        cast_out.append(pl.BlockSpec((blk, cols), dst_map))
        cast_shapes.append(jax.ShapeDtypeStruct((rows, cols), BF16))
    return pl.pallas_call(
        kernel,
        out_shape=[jax.ShapeDtypeStruct((batch, seq, ATT_W), BF16)] + cast_shapes,
        grid=(batch, n_blk),
        in_specs=[
            pl.BlockSpec(memory_space=pltpu.SMEM),
            pl.BlockSpec((None, ATT_ROWS, ATT_W), lambda b, n: (b, n, OFF_AQ // ATT_W)),
            pl.BlockSpec((None, seq, kv_w), lambda b, n: (b, 0, OFF_AK // kv_w)),
            pl.BlockSpec((None, seq, kv_w), lambda b, n: (b, 0, OFF_AV // kv_w)),
            pl.BlockSpec((N_META, kv_w), lambda b, n: (0, META_AK // kv_w)),
            pl.BlockSpec((N_META, kv_w), lambda b, n: (0, META_AV // kv_w)),
        ] + cast_in,
        out_specs=[pl.BlockSpec((None, ATT_ROWS, ATT_W), lambda b, n: (b, n, 0))] + cast_out,
        scratch_shapes=[pltpu.VMEM((N_OFFSETS, ATT_Q_HEADS, QB, N_KEYS), F32)],
        compiler_params=pltpu.CompilerParams(
            dimension_semantics=("arbitrary", "arbitrary"), vmem_limit_bytes=VMEM_LIMIT),
        name="attention",
    )(sink, proj, proj, proj, proj_meta, proj_meta, *w_casts)


FF_CHUNK = 256
GATE_HALF = D_MODEL // 2


def _mixffn_kernel(x_ref, ret_ref, rg_ref, gain_ref, att_ref, ga0_ref, ga1_ref, gb0_ref, gb1_ref,
                   wr_ref, wa_ref, wo_ref, nf_ref, wg_ref, wu_ref, wd_ref, nfin_ref, o_ref,
                   r_ref, h_ref, act_ref):
    gb = jnp.concatenate([gb0_ref[...], gb1_ref[...]], axis=1).astype(F32)
    att_part = _sigmoid(gb) * _dot(att_ref[...], wa_ref[...])
    for h in range(RET_HEADS):
        sl = slice(h * RET_DV, (h + 1) * RET_DV)
        o = ret_ref[:, sl].astype(F32)
        d = o - jnp.mean(o, axis=-1, keepdims=True)
        var = jnp.mean(d * d, axis=-1, keepdims=True)
        g = rg_ref[:, sl].astype(F32)
        r_ref[:, sl] = (g * _sigmoid(g) * (d * lax.rsqrt(var + GN_EPS) * gain_ref[:, sl])).astype(BF16)
    ga = jnp.concatenate([ga0_ref[...], ga1_ref[...]], axis=1).astype(F32)
    merged = _sigmoid(ga) * _dot(r_ref[...], wr_ref[...]) + att_part
    x = x_ref[...] + _dot(merged.astype(BF16), wo_ref[...])
    ms = jnp.mean(x * x, axis=-1, keepdims=True)
    h_ref[...] = (x * lax.rsqrt(ms + RMS_EPS) * nf_ref[...]).astype(BF16)
    for c in range(0, D_FF, FF_CHUNK):
        a = _dot(h_ref[...], wg_ref[:, c:c + FF_CHUNK])
        b = _dot(h_ref[...], wu_ref[:, c:c + FF_CHUNK])
        act_ref[:, c:c + FF_CHUNK] = (a * _sigmoid(a) * b).astype(BF16)
    y = x + _dot(act_ref[...], wd_ref[...])
    ms = jnp.mean(y * y, axis=-1, keepdims=True)
    o_ref[...] = y * lax.rsqrt(ms + RMS_EPS) * nfin_ref[...]


def _mixffn_call(x2d, ret, gn_gain, att, proj, wr, wa, wo, norm_ffn, wgu, wd, norm_final, tm):
    rows = x2d.shape[0]
    const = lambda shape, col=0: pl.BlockSpec(shape, lambda i: (0, col),
                                              pipeline_mode=pl.Buffered(1))
    row_tile = lambda col: pl.BlockSpec((tm, D_MODEL), lambda i: (i, col))
    half_tile = lambda col: pl.BlockSpec((tm, GATE_HALF), lambda i: (i, col))
    return pl.pallas_call(
        _mixffn_kernel,
        out_shape=jax.ShapeDtypeStruct((rows, D_MODEL), F32),
        grid=(rows // tm,),
        in_specs=[
            row_tile(0), row_tile(0), row_tile(OFF_RG // RET_W), const((1, RET_W)), row_tile(0),
            half_tile(OFF_GA // GATE_HALF), half_tile(OFF_GA // GATE_HALF + 1),
            half_tile(OFF_GB // GATE_HALF), half_tile(OFF_GB // GATE_HALF + 1),
            const((RET_W, D_MODEL)), const((ATT_W, D_MODEL)), const((D_MODEL, D_MODEL)),
            const((1, D_MODEL)),
            const((D_MODEL, D_FF), 0), const((D_MODEL, D_FF), 1), const((D_FF, D_MODEL)),
            const((1, D_MODEL)),
        ],
        out_specs=row_tile(0),
        scratch_shapes=[pltpu.VMEM((tm, RET_W), BF16), pltpu.VMEM((tm, D_MODEL), BF16),
                        pltpu.VMEM((tm, D_FF), BF16)],
        compiler_params=pltpu.CompilerParams(
            dimension_semantics=("parallel",), vmem_limit_bytes=VMEM_LIMIT),
        name="mixffn",
    )(x2d, ret, proj, gn_gain, att, proj, proj, proj, proj, wr, wa, wo, norm_ffn, wgu, wgu, wd,
      norm_final)


def kernel(x, meta_tokens, w_in, ret_decay_logit_fwd, ret_decay_logit_bwd, ret_gn_gain, attn_sink,
           w_branch_ret, w_branch_att, w_out, norm_mix, norm_ffn, w_gate_up, w_down, norm_final):
    batch, seq, d = x.shape
    assert d == D_MODEL and w_in.shape[0] == 1

    w_in0 = w_in[0]
    w_in_b = w_in0.astype(BF16)
    w_q_b = _pair_head_cols(w_in0[:, OFF_AQ:OFF_AK]).astype(BF16)
    nmix = norm_mix[0].reshape(1, D_MODEL)
    nffn = norm_ffn[0].reshape(1, D_MODEL)
    nfin = norm_final.reshape(1, D_MODEL)

    x2d = x.reshape(batch * seq, D_MODEL)
    proj = _inproj_call(x2d, nmix, w_in_b, w_q_b, 1024)
    proj_meta = _meta_call(meta_tokens.astype(x.dtype), nmix, w_in_b)
    proj3 = proj.reshape(batch, seq, IN_COLS)

    log_gammas = jnp.stack([jax.nn.log_sigmoid(ret_decay_logit_fwd[0].astype(F32)),
                            jax.nn.log_sigmoid(ret_decay_logit_bwd[0].astype(F32))])
    ret = _retention_call(log_gammas, proj3, proj_meta, batch, seq)

    att, wr, wa, wo, wgu, wd = _attention_call(
        attn_sink[0].astype(F32), proj3, proj_meta,
        (w_branch_ret[0], w_branch_att[0], w_out[0], w_gate_up[0], w_down[0]), batch, seq)

    out = _mixffn_call(x2d, ret.reshape(batch * seq, RET_W), ret_gn_gain[0].reshape(1, RET_W),
                       att.reshape(batch * seq, ATT_W), proj,
                       wr, wa, wo, nffn, wgu, wd, nfin, 512)
    return out.reshape(batch, seq, D_MODEL)
```

```python
import functools

import numpy as np
import jax
import jax.numpy as jnp
from jax import lax
from jax.experimental import pallas as pl
from jax.experimental.pallas import tpu as pltpu

D_MODEL = 1024
N_META = 16
RET_HEADS = 8
RET_DK = 64
RET_DV = 128
RET_W = RET_HEADS * RET_DV
ATT_Q_HEADS = 16
ATT_KV_HEADS = 4
ATT_GROUP = ATT_Q_HEADS // ATT_KV_HEADS
ATT_DH = 64
ATT_W = ATT_Q_HEADS * ATT_DH
WINDOW = 128
D_FF = 2816
IN_SIZES = (512, 512, 1024, 1024, 1024, 256, 256, 1024, 1024)
IN_COLS = sum(IN_SIZES)
OFF_RQ, OFF_RK, OFF_RV, OFF_RG, OFF_AQ, OFF_AK, OFF_AV, OFF_GA, OFF_GB = (
    int(c) for c in np.cumsum((0,) + IN_SIZES[:-1]))
RMS_EPS = 1e-6
GN_EPS = 1e-5
NEG_INF = -1e30
LOG2E = 1.4426950408889634
ATT_Q_SCALE = ATT_DH ** -0.5 * LOG2E

LANES = 128
VMEM_LIMIT = 56 * 1024 * 1024
INPROJ_ROWS = 1024
MIXFFN_ROWS = 512

F32 = jnp.float32
BF16 = jnp.bfloat16


def _sigmoid(x):
    return 1.0 / (1.0 + jnp.exp(-x))


def _dot(a, b):
    return jnp.dot(a, b, preferred_element_type=F32)


def _dot_nt(a, b):
    return lax.dot_general(a, b, (((1,), (1,)), ((), ())), preferred_element_type=F32)


def _dot_tn(a, b):
    return lax.dot_general(a, b, (((0,), (0,)), ((), ())), preferred_element_type=F32)


IN_CHUNK = 512
IN_SUB = 512


def _inproj_kernel(x_ref, g_ref, w_ref, wq_ref, o_ref, h_ref):
    tm = x_ref.shape[0]
    sub = min(IN_SUB, tm)
    for r0 in range(0, tm, sub):
        rows = slice(r0, r0 + sub)
        x = x_ref[rows, :]
        ms = jnp.mean(x * x, axis=-1, keepdims=True)
        h_ref[rows, :] = (x * lax.rsqrt(ms + RMS_EPS) * g_ref[...]).astype(BF16)
        for c in range(0, IN_COLS, IN_CHUNK):
            if OFF_AQ <= c < OFF_AQ + ATT_W:
                q0 = c - OFF_AQ
                acc = _dot(h_ref[rows, :], wq_ref[:, q0:q0 + IN_CHUNK]) * ATT_Q_SCALE
            else:
                acc = _dot(h_ref[rows, :], w_ref[:, c:c + IN_CHUNK])
            o_ref[rows, c:c + IN_CHUNK] = acc.astype(BF16)


def _inproj_call(x2d, gain, w_bf16, wq_bf16, tm):
    rows = x2d.shape[0]
    const = lambda shape: pl.BlockSpec(shape, lambda i: (0, 0), pipeline_mode=pl.Buffered(1))
    return pl.pallas_call(
        _inproj_kernel,
        out_shape=jax.ShapeDtypeStruct((rows, IN_COLS), BF16),
        grid=(rows // tm,),
        in_specs=[
            pl.BlockSpec((tm, D_MODEL), lambda i: (i, 0)),
            pl.BlockSpec((1, D_MODEL), lambda i: (0, 0)),
            const((D_MODEL, IN_COLS)), const((D_MODEL, ATT_W)),
        ],
        out_specs=pl.BlockSpec((tm, IN_COLS), lambda i: (i, 0)),
        scratch_shapes=[pltpu.VMEM((tm, D_MODEL), BF16)],
        compiler_params=pltpu.CompilerParams(
            dimension_semantics=("parallel",), vmem_limit_bytes=VMEM_LIMIT),
        name="inproj",
    )(x2d, gain, w_bf16, wq_bf16)


META_RK = 0
META_RV = META_RK + RET_HEADS * RET_DK
META_AK = META_RV + RET_W
META_AV = META_AK + ATT_KV_HEADS * ATT_DH
META_COLS = META_AV + ATT_KV_HEADS * ATT_DH


def _meta_kernel(x_ref, g_ref, wk_ref, wv_ref, wa_ref, o_ref):
    x = x_ref[...]
    ms = jnp.mean(x * x, axis=-1, keepdims=True)
    h = (x * lax.rsqrt(ms + RMS_EPS) * g_ref[...]).astype(BF16)
    o_ref[:, META_RK:META_RV] = _dot(h, wk_ref[...]).astype(BF16)
    o_ref[:, META_RV:META_AK] = _dot(h, wv_ref[...]).astype(BF16)
    o_ref[:, META_AK:META_COLS] = _dot(h, wa_ref[...]).astype(BF16)


def _meta_call(meta, gain, w_bf16):
    kv_w = ATT_KV_HEADS * ATT_DH
    assert OFF_AV == OFF_AK + kv_w
    piece = lambda width, off: pl.BlockSpec((D_MODEL, width), lambda i: (0, off // width))
    return pl.pallas_call(
        _meta_kernel,
        out_shape=jax.ShapeDtypeStruct((N_META, META_COLS), BF16),
        grid=(1,),
        in_specs=[
            pl.BlockSpec((N_META, D_MODEL), lambda i: (0, 0)),
            pl.BlockSpec((1, D_MODEL), lambda i: (0, 0)),
            piece(RET_HEADS * RET_DK, OFF_RK), piece(RET_W, OFF_RV), piece(2 * kv_w, OFF_AK),
        ],
        out_specs=pl.BlockSpec((N_META, META_COLS), lambda i: (0, 0)),
        name="metaproj",
    )(meta, gain, w_bf16, w_bf16, w_bf16)


RET_C = 256
RET_BATCH = 4


def _retention_kernel(lg_ref, q_ref, k_ref, v_ref, km_ref, vm_ref, o_ref,
                      tq_ref, tk_ref, td_ref, sel_ref, ts_ref, tm_ref, u_ref, s_ref, i_ref,
                      *, n_chunks):
    pair = pl.program_id(0)
    C = RET_C
    k_scale = RET_DK ** -0.5

    @pl.when(pl.program_id(1) == 0)
    def _build_tables():
        lgf = [lg_ref[0, 2 * pair + hl] for hl in range(2)]
        lgb = [lg_ref[1, 2 * pair + hl] for hl in range(2)]
        row = lax.broadcasted_iota(jnp.int32, (C, LANES), 0).astype(F32)
        first = lax.broadcasted_iota(jnp.int32, (C, LANES), 1) < RET_DK
        pos_t = lax.broadcasted_iota(jnp.int32, (LANES, C), 1).astype(F32)
        first_t = lax.broadcasted_iota(jnp.int32, (LANES, C), 0) < RET_DK
        diff = (lax.broadcasted_iota(jnp.int32, (C, C), 0)
                - lax.broadcasted_iota(jnp.int32, (C, C), 1)).astype(F32)
        srow = lax.broadcasted_iota(jnp.int32, (LANES, 2 * RET_DV), 0)
        scol = lax.broadcasted_iota(jnp.int32, (LANES, 2 * RET_DV), 1)
        mrow = lax.broadcasted_iota(jnp.int32, (N_META, LANES), 0).astype(F32)
        mfirst = lax.broadcasted_iota(jnp.int32, (N_META, LANES), 1) < RET_DK

        def both(sel, fn, lg):
            return jnp.where(sel, fn(lg[0]), fn(lg[1]))

        tq_ref[0] = both(first, lambda l: jnp.exp((row + 1.0) * l), lgf)
        tq_ref[1] = both(first, lambda l: jnp.exp((C - row) * l), lgb)
        tk_ref[0] = both(first_t, lambda l: jnp.exp((C - 1.0 - pos_t) * l), lgf) * k_scale
        tk_ref[1] = both(first_t, lambda l: jnp.exp(pos_t * l), lgb) * k_scale
        tm_ref[...] = both(mfirst, lambda l: jnp.exp((N_META - 1.0 - mrow) * l), lgf) * k_scale
        for hl in range(2):
            td_ref[hl] = k_scale * jnp.where(diff >= 0, jnp.exp(jnp.maximum(diff, 0.0) * lgf[hl]),
                                             jnp.exp(jnp.maximum(-diff, 0.0) * lgb[hl]))
            sel_ref[hl] = jnp.where(first == (hl == 0), 1.0, 0.0).astype(BF16)
        own = (srow < RET_DK) == (scol < RET_DV)
        zeros = jnp.zeros((LANES, 2 * RET_DV), F32)
        ts_ref[0] = jnp.where(own, both(srow < RET_DK, lambda l: jnp.exp(zeros + C * l), lgf), 0.0)
        ts_ref[1] = jnp.where(own, both(srow < RET_DK, lambda l: jnp.exp(zeros + C * l), lgb), 0.0)
        ts_ref[2] = jnp.where(own, 1.0, 0.0)

    def head(x, hl):
        return x[:, hl * RET_DV:(hl + 1) * RET_DV]

    own_mask = ts_ref[2]
    for bb in range(RET_BATCH):
        _retention_one(q_ref.at[bb], k_ref.at[bb], v_ref.at[bb], km_ref, vm_ref, o_ref.at[bb],
                       tq_ref, tk_ref, td_ref, sel_ref, ts_ref, tm_ref, u_ref.at[bb], s_ref.at[bb],
                       i_ref.at[bb], own_mask, head, n_chunks)


def _retention_one(q_ref, k_ref, v_ref, km_ref, vm_ref, o_ref, tq_ref, tk_ref, td_ref, sel_ref,
                   ts_ref, tm_ref, u_ref, s_ref, i_ref, own_mask, head, n_chunks):
    C = RET_C

    for c in range(n_chunks):
        rows = slice(c * C, (c + 1) * C)
        kc = k_ref[rows, :]
        vc = v_ref[rows, :]
        k_t = kc.astype(F32).T
        lhs = jnp.concatenate([k_t * tk_ref[0], k_t * tk_ref[1]], axis=0).astype(BF16)
        u = _dot(lhs, vc)
        u_ref[c, 0:LANES, :] = u[0:LANES] * own_mask
        u_ref[c, LANES:2 * LANES, :] = u[LANES:2 * LANES] * own_mask
        q16 = q_ref[rows, :]
        for hl in range(2):
            s = _dot_nt(q16 * sel_ref[hl], kc) * td_ref[hl]
            i_ref[c, :, hl * RET_DV:(hl + 1) * RET_DV] = _dot(s.astype(BF16), head(vc, hl))

    sf = _dot_tn((km_ref[...].astype(F32) * tm_ref[...]).astype(BF16), vm_ref[...]) * own_mask
    for c in range(n_chunks):
        s_ref[c, 0:LANES, :] = sf.astype(BF16)
        if c + 1 < n_chunks:
            sf = sf * ts_ref[0] + u_ref[c, 0:LANES, :]
    sb = jnp.zeros((LANES, 2 * RET_DV), F32)
    for c in reversed(range(n_chunks)):
        s_ref[c, LANES:2 * LANES, :] = sb.astype(BF16)
        if c > 0:
            sb = sb * ts_ref[1] + u_ref[c, LANES:2 * LANES, :]

    for c in range(n_chunks):
        rows = slice(c * C, (c + 1) * C)
        qc = q_ref[rows, :].astype(F32)
        q_cross = jnp.concatenate([(qc * tq_ref[0]).astype(BF16),
                                   (qc * tq_ref[1]).astype(BF16)], axis=1)
        o_ref[rows, :] = (i_ref[c] + _dot(q_cross, s_ref[c])).astype(BF16)


def _retention_call(log_gammas, proj, proj_meta, batch, seq):
    assert seq % RET_C == 0 and batch % RET_BATCH == 0
    n_chunks = seq // RET_C
    n_pairs = RET_HEADS // 2
    kernel = functools.partial(_retention_kernel, n_chunks=n_chunks)
    qb, kb = OFF_RQ // LANES, OFF_RK // LANES
    vb = OFF_RV // (2 * RET_DV)
    return pl.pallas_call(
        kernel,
        out_shape=jax.ShapeDtypeStruct((batch, seq, RET_W), BF16),
        grid=(n_pairs, batch // RET_BATCH),
        in_specs=[
            pl.BlockSpec(memory_space=pltpu.SMEM),
            pl.BlockSpec((RET_BATCH, seq, LANES), lambda p, b: (b, 0, qb + p)),
            pl.BlockSpec((RET_BATCH, seq, LANES), lambda p, b: (b, 0, kb + p)),
            pl.BlockSpec((RET_BATCH, seq, 2 * RET_DV), lambda p, b: (b, 0, vb + p)),
            pl.BlockSpec((N_META, LANES), lambda p, b: (0, META_RK // LANES + p)),
            pl.BlockSpec((N_META, 2 * RET_DV), lambda p, b: (0, META_RV // (2 * RET_DV) + p)),
        ],
        out_specs=pl.BlockSpec((RET_BATCH, seq, 2 * RET_DV), lambda p, b: (b, 0, p)),
        scratch_shapes=[
            pltpu.VMEM((2, RET_C, LANES), F32),
            pltpu.VMEM((2, LANES, RET_C), F32),
            pltpu.VMEM((2, RET_C, RET_C), F32),
            pltpu.VMEM((2, RET_C, LANES), BF16),
            pltpu.VMEM((3, LANES, 2 * RET_DV), F32),
            pltpu.VMEM((N_META, LANES), F32),
            pltpu.VMEM((RET_BATCH, n_chunks, 2 * LANES, 2 * RET_DV), F32),
            pltpu.VMEM((RET_BATCH, n_chunks, 2 * LANES, 2 * RET_DV), BF16),
            pltpu.VMEM((RET_BATCH, n_chunks, RET_C, 2 * RET_DV), F32),
        ],
        compiler_params=pltpu.CompilerParams(
            dimension_semantics=("arbitrary", "arbitrary"), vmem_limit_bytes=VMEM_LIMIT),
        name="retention",
    )(log_gammas, proj, proj, proj, proj_meta, proj_meta)


def _pair_head_cols(w):
    w5 = w.reshape(w.shape[0], ATT_KV_HEADS // 2, 2, ATT_GROUP, ATT_DH)
    return w5.transpose(0, 1, 3, 2, 4).reshape(w.shape)


QB = 64
BAND = QB + 2 * WINDOW
META_PAD = 64
N_KEYS = BAND + META_PAD
ATT_ROWS = 2048
N_OFFSETS = 2 * WINDOW // QB + 1
SINK_COL = BAND + N_META
HB_UNROLL = 16


def _attention_kernel(sink_ref, q_ref, k_ref, v_ref, km_ref, vm_ref, *rest, seq, n_cast):
    cast_src, o_ref = rest[:n_cast], rest[n_cast]
    cast_dst, bias_ref = rest[n_cast + 1:2 * n_cast + 1], rest[2 * n_cast + 1]
    for src, dst in zip(cast_src, cast_dst):
        dst[...] = src[...].astype(BF16)
    _attention_body(sink_ref, q_ref, k_ref, v_ref, km_ref, vm_ref, o_ref, bias_ref, seq=seq)


def _attention_body(sink_ref, q_ref, k_ref, v_ref, km_ref, vm_ref, o_ref, bias_ref, *, seq):
    n = pl.program_id(1)

    @pl.when((pl.program_id(0) == 0) & (n == 0))
    def _build_bias():
        r = lax.broadcasted_iota(jnp.int32, (QB, N_KEYS), 0)
        j = lax.broadcasted_iota(jnp.int32, (QB, N_KEYS), 1)
        for t in range(N_OFFSETS):
            dist = jnp.abs(j - t * QB - r)
            band_ok = (j < BAND) & (dist <= WINDOW)
            ndist = dist.astype(F32) * (-LOG2E)
            for head in range(ATT_Q_HEADS):
                slope = 2.0 ** (-8.0 * (head + 1) / ATT_Q_HEADS)
                sink = sink_ref[head] * LOG2E
                rest = jnp.where((j >= BAND) & (j < SINK_COL), 0.0,
                                 jnp.where(j == SINK_COL, sink, NEG_INF))
                bias_ref[t, head] = jnp.where(band_ok, slope * ndist, rest)

    low_k = lax.broadcasted_iota(jnp.int32, (N_KEYS, LANES), 1) < ATT_DH

    def block_diag(x):
        zero = jnp.zeros_like(x)
        return jnp.concatenate([jnp.where(low_k, x, zero), jnp.where(low_k, zero, x)], axis=0)

    ones_bd = jnp.concatenate([jnp.where(low_k, 1.0, 0.0), jnp.where(low_k, 0.0, 1.0)],
                              axis=0).astype(BF16)
    meta_fill = jnp.zeros((META_PAD - N_META, LANES), BF16)

    def body(it, carry):
        for u in range(HB_UNROLL):
            hb = it * HB_UNROLL + u
            row0 = n * ATT_ROWS + hb * QB
            start = pl.multiple_of(jnp.clip(row0 - WINDOW, 0, seq - BAND), QB)
            t = (row0 - start) // QB
            rows = pl.ds(pl.multiple_of(hb * QB, QB), QB)
            for jp in range(ATT_KV_HEADS // 2):
                ksl = slice(jp * LANES, (jp + 1) * LANES)
                k_bd = block_diag(jnp.concatenate(
                    [k_ref[pl.ds(start, BAND), ksl], km_ref[:, ksl], meta_fill], axis=0))
                v_bd = jnp.concatenate([block_diag(jnp.concatenate(
                    [v_ref[pl.ds(start, BAND), ksl], vm_ref[:, ksl], meta_fill], axis=0)),
                    ones_bd], axis=1)
                groups = [slice((jp * ATT_GROUP + g) * LANES, (jp * ATT_GROUP + g + 1) * LANES)
                          for g in range(ATT_GROUP)]
                q_stack = jnp.concatenate([q_ref[rows, grp] for grp in groups], axis=0)
                s = _dot_nt(q_stack, k_bd)
                probs = []
                for g in range(ATT_GROUP):
                    p_halves = []
                    for half in range(2):
                        head = (2 * jp + half) * ATT_GROUP + g
                        sg = (s[g * QB:(g + 1) * QB, half * N_KEYS:(half + 1) * N_KEYS]
                              + bias_ref[t, head])
                        p = jnp.exp2(sg - jnp.max(sg, axis=-1, keepdims=True))
                        p_halves.append(p.astype(BF16))
                    probs.append(jnp.concatenate(p_halves, axis=1))
                o = _dot(jnp.concatenate(probs, axis=0), v_bd)
                out = o[:, 0:LANES] / o[:, LANES:2 * LANES]
                for g in range(ATT_GROUP):
                    o_ref[rows, groups[g]] = out[g * QB:(g + 1) * QB].astype(BF16)
        return carry

    lax.fori_loop(0, ATT_ROWS // QB // HB_UNROLL, body, 0)


def _paired_src_block(i, blocks_per_group):
    grp, sub = i // blocks_per_group, i % blocks_per_group
    j, g, half = grp // (2 * ATT_GROUP), (grp // 2) % ATT_GROUP, grp % 2
    return ((j * 2 + half) * ATT_GROUP + g) * blocks_per_group + sub


def _attention_call(sink, proj, proj_meta, w_casts, batch, seq):
    assert seq % ATT_ROWS == 0 and seq >= BAND
    n_blk = seq // ATT_ROWS
    steps = batch * n_blk
    kernel = functools.partial(_attention_kernel, seq=seq, n_cast=len(w_casts))
    kv_w = ATT_KV_HEADS * ATT_DH
    step = lambda b, n: b * n_blk + n
    cast_in, cast_out, cast_shapes = [], [], []
    for idx, w in enumerate(w_casts):
        rows, cols = w.shape
        reps = 1
        while rows % (steps // reps) or (rows // (steps // reps)) % 16:
            reps *= 2
        blk = rows // (steps // reps)
        dst_map = lambda b, n, reps=reps: (step(b, n) // reps, 0)
        if idx == 1:
            assert reps == 1 and ATT_DH % blk == 0
            src_map = lambda b, n, bpg=ATT_DH // blk: (_paired_src_block(step(b, n), bpg), 0)
        else:
            src_map = dst_map
        cast_in.append(pl.BlockSpec((blk, cols), src_map))
        cast_out.append(pl.BlockSpec((blk, cols), dst_map))
        cast_shapes.append(jax.ShapeDtypeStruct((rows, cols), BF16))
    return pl.pallas_call(
        kernel,
        out_shape=[jax.ShapeDtypeStruct((batch, seq, ATT_W), BF16)] + cast_shapes,
        grid=(batch, n_blk),
        in_specs=[
            pl.BlockSpec(memory_space=pltpu.SMEM),
            pl.BlockSpec((None, ATT_ROWS, ATT_W), lambda b, n: (b, n, OFF_AQ // ATT_W)),
            pl.BlockSpec((None, seq, kv_w), lambda b, n: (b, 0, OFF_AK // kv_w)),
            pl.BlockSpec((None, seq, kv_w), lambda b, n: (b, 0, OFF_AV // kv_w)),
            pl.BlockSpec((N_META, kv_w), lambda b, n: (0, META_AK // kv_w)),
            pl.BlockSpec((N_META, kv_w), lambda b, n: (0, META_AV // kv_w)),
        ] + cast_in,
        out_specs=[pl.BlockSpec((None, ATT_ROWS, ATT_W), lambda b, n: (b, n, 0))] + cast_out,
        scratch_shapes=[pltpu.VMEM((N_OFFSETS, ATT_Q_HEADS, QB, N_KEYS), F32)],
        compiler_params=pltpu.CompilerParams(
            dimension_semantics=("arbitrary", "arbitrary"), vmem_limit_bytes=VMEM_LIMIT),
        name="attention",
    )(sink, proj, proj, proj, proj_meta, proj_meta, *w_casts)


FF_CHUNK = 256
GATE_HALF = D_MODEL // 2


def _mixffn_kernel(x_ref, ret_ref, rg_ref, gain_ref, att_ref, ga0_ref, ga1_ref, gb0_ref, gb1_ref,
                   wr_ref, wa_ref, wo_ref, nf_ref, wg_ref, wu_ref, wd_ref, nfin_ref, o_ref,
                   r_ref, h_ref, act_ref):
    gb = jnp.concatenate([gb0_ref[...], gb1_ref[...]], axis=1).astype(F32)
    att_part = _sigmoid(gb) * _dot(att_ref[...], wa_ref[...])
    ret_part = None
    for h in range(RET_HEADS):
        sl = slice(h * RET_DV, (h + 1) * RET_DV)
        o = ret_ref[:, sl].astype(F32)
        d = o - jnp.mean(o, axis=-1, keepdims=True)
        var = jnp.mean(d * d, axis=-1, keepdims=True)
        g = rg_ref[:, sl].astype(F32)
        r_ref[:, sl] = (g * _sigmoid(g) * (d * lax.rsqrt(var + GN_EPS) * gain_ref[:, sl])).astype(BF16)
        if h % 2 == 1:
            ks = slice((h - 1) * RET_DV, (h + 1) * RET_DV)
            part = _dot(r_ref[:, ks], wr_ref[ks, :])
            ret_part = part if ret_part is None else ret_part + part
    ga = jnp.concatenate([ga0_ref[...], ga1_ref[...]], axis=1).astype(F32)
    merged = _sigmoid(ga) * ret_part + att_part
    x = x_ref[...] + _dot(merged.astype(BF16), wo_ref[...])
    ms = jnp.mean(x * x, axis=-1, keepdims=True)
    h_ref[...] = (x * lax.rsqrt(ms + RMS_EPS) * nf_ref[...]).astype(BF16)
    for c in range(0, D_FF, FF_CHUNK):
        a = _dot(h_ref[...], wg_ref[:, c:c + FF_CHUNK])
        b = _dot(h_ref[...], wu_ref[:, c:c + FF_CHUNK])
        act_ref[:, c:c + FF_CHUNK] = (a * _sigmoid(a) * b).astype(BF16)
    y = x + _dot(act_ref[...], wd_ref[...])
    ms = jnp.mean(y * y, axis=-1, keepdims=True)
    o_ref[...] = y * lax.rsqrt(ms + RMS_EPS) * nfin_ref[...]


def _mixffn_call(x2d, ret, gn_gain, att, proj, wr, wa, wo, norm_ffn, wgu, wd, norm_final, tm):
    rows = x2d.shape[0]
    const = lambda shape, col=0: pl.BlockSpec(shape, lambda i: (0, col),
                                              pipeline_mode=pl.Buffered(1))
    row_tile = lambda col: pl.BlockSpec((tm, D_MODEL), lambda i: (i, col))
    half_tile = lambda col: pl.BlockSpec((tm, GATE_HALF), lambda i: (i, col))
    return pl.pallas_call(
        _mixffn_kernel,
        out_shape=jax.ShapeDtypeStruct((rows, D_MODEL), F32),
        grid=(rows // tm,),
        in_specs=[
            row_tile(0), row_tile(0), row_tile(OFF_RG // RET_W), const((1, RET_W)), row_tile(0),
            half_tile(OFF_GA // GATE_HALF), half_tile(OFF_GA // GATE_HALF + 1),
            half_tile(OFF_GB // GATE_HALF), half_tile(OFF_GB // GATE_HALF + 1),
            const((RET_W, D_MODEL)), const((ATT_W, D_MODEL)), const((D_MODEL, D_MODEL)),
            const((1, D_MODEL)),
            const((D_MODEL, D_FF), 0), const((D_MODEL, D_FF), 1), const((D_FF, D_MODEL)),
            const((1, D_MODEL)),
        ],
        out_specs=row_tile(0),
        scratch_shapes=[pltpu.VMEM((tm, RET_W), BF16), pltpu.VMEM((tm, D_MODEL), BF16),
                        pltpu.VMEM((tm, D_FF), BF16)],
        compiler_params=pltpu.CompilerParams(
            dimension_semantics=("parallel",), vmem_limit_bytes=VMEM_LIMIT),
        name="mixffn",
    )(x2d, ret, proj, gn_gain, att, proj, proj, proj, proj, wr, wa, wo, norm_ffn, wgu, wgu, wd,
      norm_final)


def kernel(x, meta_tokens, w_in, ret_decay_logit_fwd, ret_decay_logit_bwd, ret_gn_gain, attn_sink,
           w_branch_ret, w_branch_att, w_out, norm_mix, norm_ffn, w_gate_up, w_down, norm_final):
    batch, seq, d = x.shape
    assert d == D_MODEL and w_in.shape[0] == 1

    w_in0 = w_in[0]
    w_in_b = w_in0.astype(BF16)
    w_q_b = _pair_head_cols(w_in0[:, OFF_AQ:OFF_AK]).astype(BF16)
    nmix = norm_mix[0].reshape(1, D_MODEL)
    nffn = norm_ffn[0].reshape(1, D_MODEL)
    nfin = norm_final.reshape(1, D_MODEL)

    x2d = x.reshape(batch * seq, D_MODEL)
    proj = _inproj_call(x2d, nmix, w_in_b, w_q_b, INPROJ_ROWS)
    proj_meta = _meta_call(meta_tokens.astype(x.dtype), nmix, w_in_b)
    proj3 = proj.reshape(batch, seq, IN_COLS)

    log_gammas = jnp.stack([jax.nn.log_sigmoid(ret_decay_logit_fwd[0].astype(F32)),
                            jax.nn.log_sigmoid(ret_decay_logit_bwd[0].astype(F32))])
    ret = _retention_call(log_gammas, proj3, proj_meta, batch, seq)

    att, wr, wa, wo, wgu, wd = _attention_call(
        attn_sink[0].astype(F32), proj3, proj_meta,
        (w_branch_ret[0], w_branch_att[0], w_out[0], w_gate_up[0], w_down[0]), batch, seq)

    out = _mixffn_call(x2d, ret.reshape(batch * seq, RET_W), ret_gn_gain[0].reshape(1, RET_W),
                       att.reshape(batch * seq, ATT_W), proj,
                       wr, wa, wo, nffn, wgu, wd, nfin, MIXFFN_ROWS)
    return out.reshape(batch, seq, D_MODEL)
```

```python
import functools

import numpy as np
import jax
import jax.numpy as jnp
from jax import lax
from jax.experimental import pallas as pl
from jax.experimental.pallas import tpu as pltpu

D_MODEL = 1024
N_META = 16
RET_HEADS = 8
RET_DK = 64
RET_DV = 128
RET_W = RET_HEADS * RET_DV
ATT_Q_HEADS = 16
ATT_KV_HEADS = 4
ATT_GROUP = ATT_Q_HEADS // ATT_KV_HEADS
ATT_DH = 64
ATT_W = ATT_Q_HEADS * ATT_DH
WINDOW = 128
D_FF = 2816
IN_SIZES = (512, 512, 1024, 1024, 1024, 256, 256, 1024, 1024)
IN_COLS = sum(IN_SIZES)
OFF_RQ, OFF_RK, OFF_RV, OFF_RG, OFF_AQ, OFF_AK, OFF_AV, OFF_GA, OFF_GB = (
    int(c) for c in np.cumsum((0,) + IN_SIZES[:-1]))
RMS_EPS = 1e-6
GN_EPS = 1e-5
NEG_INF = -1e30
LOG2E = 1.4426950408889634
ATT_Q_SCALE = ATT_DH ** -0.5 * LOG2E

LANES = 128
VMEM_LIMIT = 56 * 1024 * 1024
INPROJ_ROWS = 1024
MIXFFN_ROWS = 512

F32 = jnp.float32
BF16 = jnp.bfloat16


def _sigmoid(x):
    return 1.0 / (1.0 + jnp.exp(-x))


def _dot(a, b):
    return jnp.dot(a, b, preferred_element_type=F32)


def _dot_nt(a, b):
    return lax.dot_general(a, b, (((1,), (1,)), ((), ())), preferred_element_type=F32)


def _dot_tn(a, b):
    return lax.dot_general(a, b, (((0,), (0,)), ((), ())), preferred_element_type=F32)


IN_CHUNK = 512
IN_SUB = 512


def _inproj_kernel(x_ref, g_ref, w_ref, wq_ref, o_ref, h_ref):
    tm = x_ref.shape[0]
    sub = min(IN_SUB, tm)
    for r0 in range(0, tm, sub):
        rows = slice(r0, r0 + sub)
        x = x_ref[rows, :]
        ms = jnp.mean(x * x, axis=-1, keepdims=True)
        h_ref[rows, :] = (x * lax.rsqrt(ms + RMS_EPS) * g_ref[...]).astype(BF16)
        for c in range(0, IN_COLS, IN_CHUNK):
            if OFF_AQ <= c < OFF_AQ + ATT_W:
                q0 = c - OFF_AQ
                acc = _dot(h_ref[rows, :], wq_ref[:, q0:q0 + IN_CHUNK]) * ATT_Q_SCALE
            else:
                acc = _dot(h_ref[rows, :], w_ref[:, c:c + IN_CHUNK])
            o_ref[rows, c:c + IN_CHUNK] = acc.astype(BF16)


def _inproj_call(x2d, gain, w_bf16, wq_bf16, tm):
    rows = x2d.shape[0]
    const = lambda shape: pl.BlockSpec(shape, lambda i: (0, 0), pipeline_mode=pl.Buffered(1))
    return pl.pallas_call(
        _inproj_kernel,
        out_shape=jax.ShapeDtypeStruct((rows, IN_COLS), BF16),
        grid=(rows // tm,),
        in_specs=[
            pl.BlockSpec((tm, D_MODEL), lambda i: (i, 0)),
            pl.BlockSpec((1, D_MODEL), lambda i: (0, 0)),
            const((D_MODEL, IN_COLS)), const((D_MODEL, ATT_W)),
        ],
        out_specs=pl.BlockSpec((tm, IN_COLS), lambda i: (i, 0)),
        scratch_shapes=[pltpu.VMEM((tm, D_MODEL), BF16)],
        compiler_params=pltpu.CompilerParams(
            dimension_semantics=("parallel",), vmem_limit_bytes=VMEM_LIMIT),
        name="inproj",
    )(x2d, gain, w_bf16, wq_bf16)


META_RK = 0
META_RV = META_RK + RET_HEADS * RET_DK
META_AK = META_RV + RET_W
META_AV = META_AK + ATT_KV_HEADS * ATT_DH
META_COLS = META_AV + ATT_KV_HEADS * ATT_DH


def _meta_kernel(x_ref, g_ref, wk_ref, wv_ref, wa_ref, o_ref):
    x = x_ref[...]
    ms = jnp.mean(x * x, axis=-1, keepdims=True)
    h = (x * lax.rsqrt(ms + RMS_EPS) * g_ref[...]).astype(BF16)
    o_ref[:, META_RK:META_RV] = _dot(h, wk_ref[...]).astype(BF16)
    o_ref[:, META_RV:META_AK] = _dot(h, wv_ref[...]).astype(BF16)
    o_ref[:, META_AK:META_COLS] = _dot(h, wa_ref[...]).astype(BF16)


def _meta_call(meta, gain, w_bf16):
    kv_w = ATT_KV_HEADS * ATT_DH
    assert OFF_AV == OFF_AK + kv_w
    piece = lambda width, off: pl.BlockSpec((D_MODEL, width), lambda i: (0, off // width))
    return pl.pallas_call(
        _meta_kernel,
        out_shape=jax.ShapeDtypeStruct((N_META, META_COLS), BF16),
        grid=(1,),
        in_specs=[
            pl.BlockSpec((N_META, D_MODEL), lambda i: (0, 0)),
            pl.BlockSpec((1, D_MODEL), lambda i: (0, 0)),
            piece(RET_HEADS * RET_DK, OFF_RK), piece(RET_W, OFF_RV), piece(2 * kv_w, OFF_AK),
        ],
        out_specs=pl.BlockSpec((N_META, META_COLS), lambda i: (0, 0)),
        name="metaproj",
    )(meta, gain, w_bf16, w_bf16, w_bf16)


RET_C = 256
RET_BATCH = 4


def _retention_kernel(lg_ref, q_ref, k_ref, v_ref, km_ref, vm_ref, o_ref,
                      tq_ref, tk_ref, td_ref, sel_ref, ts_ref, tm_ref, u_ref, s_ref, i_ref,
                      *, n_chunks):
    pair = pl.program_id(0)
    C = RET_C
    k_scale = RET_DK ** -0.5

    @pl.when(pl.program_id(1) == 0)
    def _build_tables():
        lgf = [lg_ref[0, 2 * pair + hl] for hl in range(2)]
        lgb = [lg_ref[1, 2 * pair + hl] for hl in range(2)]
        row = lax.broadcasted_iota(jnp.int32, (C, LANES), 0).astype(F32)
        first = lax.broadcasted_iota(jnp.int32, (C, LANES), 1) < RET_DK
        pos_t = lax.broadcasted_iota(jnp.int32, (LANES, C), 1).astype(F32)
        first_t = lax.broadcasted_iota(jnp.int32, (LANES, C), 0) < RET_DK
        diff = (lax.broadcasted_iota(jnp.int32, (C, C), 0)
                - lax.broadcasted_iota(jnp.int32, (C, C), 1)).astype(F32)
        srow = lax.broadcasted_iota(jnp.int32, (LANES, 2 * RET_DV), 0)
        scol = lax.broadcasted_iota(jnp.int32, (LANES, 2 * RET_DV), 1)
        mrow = lax.broadcasted_iota(jnp.int32, (N_META, LANES), 0).astype(F32)
        mfirst = lax.broadcasted_iota(jnp.int32, (N_META, LANES), 1) < RET_DK

        def both(sel, fn, lg):
            return jnp.where(sel, fn(lg[0]), fn(lg[1]))

        tq_ref[0] = both(first, lambda l: jnp.exp((row + 1.0) * l), lgf)
        tq_ref[1] = both(first, lambda l: jnp.exp((C - row) * l), lgb)
        tk_ref[0] = both(first_t, lambda l: jnp.exp((C - 1.0 - pos_t) * l), lgf) * k_scale
        tk_ref[1] = both(first_t, lambda l: jnp.exp(pos_t * l), lgb) * k_scale
        tm_ref[...] = both(mfirst, lambda l: jnp.exp((N_META - 1.0 - mrow) * l), lgf) * k_scale
        for hl in range(2):
            td_ref[hl] = k_scale * jnp.where(diff <= 0, jnp.exp(jnp.maximum(-diff, 0.0) * lgf[hl]),
                                             jnp.exp(jnp.maximum(diff, 0.0) * lgb[hl]))
            sel_ref[hl] = jnp.where(first == (hl == 0), 1.0, 0.0).astype(BF16)
        own = (srow < RET_DK) == (scol < RET_DV)
        zeros = jnp.zeros((LANES, 2 * RET_DV), F32)
        ts_ref[0] = jnp.where(own, both(srow < RET_DK, lambda l: jnp.exp(zeros + C * l), lgf), 0.0)
        ts_ref[1] = jnp.where(own, both(srow < RET_DK, lambda l: jnp.exp(zeros + C * l), lgb), 0.0)
        ts_ref[2] = jnp.where(own, 1.0, 0.0)

    def head(x, hl):
        return x[:, hl * RET_DV:(hl + 1) * RET_DV]

    own_mask = ts_ref[2]
    for bb in range(RET_BATCH):
        _retention_one(q_ref.at[bb], k_ref.at[bb], v_ref.at[bb], km_ref, vm_ref, o_ref.at[bb],
                       tq_ref, tk_ref, td_ref, sel_ref, ts_ref, tm_ref, u_ref.at[bb], s_ref.at[bb],
                       i_ref.at[bb], own_mask, head, n_chunks)


def _retention_one(q_ref, k_ref, v_ref, km_ref, vm_ref, o_ref, tq_ref, tk_ref, td_ref, sel_ref,
                   ts_ref, tm_ref, u_ref, s_ref, i_ref, own_mask, head, n_chunks):
    C = RET_C

    for c in range(n_chunks):
        rows = slice(c * C, (c + 1) * C)
        kc = k_ref[rows, :]
        vc = v_ref[rows, :]
        k_t = kc.astype(F32).T
        lhs = jnp.concatenate([k_t * tk_ref[0], k_t * tk_ref[1]], axis=0).astype(BF16)
        u = _dot(lhs, vc)
        u_ref[c, 0:LANES, :] = u[0:LANES] * own_mask
        u_ref[c, LANES:2 * LANES, :] = u[LANES:2 * LANES] * own_mask
        q16 = q_ref[rows, :]
        for hl in range(2):
            s_t = _dot_nt(kc, q16 * sel_ref[hl]) * td_ref[hl]
            o_t = _dot_tn(head(vc, hl), s_t.astype(BF16))
            i_ref[c, :, hl * RET_DV:(hl + 1) * RET_DV] = o_t.T

    sf = _dot_tn((km_ref[...].astype(F32) * tm_ref[...]).astype(BF16), vm_ref[...]) * own_mask
    for c in range(n_chunks):
        s_ref[c, 0:LANES, :] = sf.astype(BF16)
        if c + 1 < n_chunks:
            sf = sf * ts_ref[0] + u_ref[c, 0:LANES, :]
    sb = jnp.zeros((LANES, 2 * RET_DV), F32)
    for c in reversed(range(n_chunks)):
        s_ref[c, LANES:2 * LANES, :] = sb.astype(BF16)
        if c > 0:
            sb = sb * ts_ref[1] + u_ref[c, LANES:2 * LANES, :]

    for c in range(n_chunks):
        rows = slice(c * C, (c + 1) * C)
        qc = q_ref[rows, :].astype(F32)
        q_cross = jnp.concatenate([(qc * tq_ref[0]).astype(BF16),
                                   (qc * tq_ref[1]).astype(BF16)], axis=1)
        o_ref[rows, :] = (i_ref[c] + _dot(q_cross, s_ref[c])).astype(BF16)


def _retention_call(log_gammas, proj, proj_meta, batch, seq):
    assert seq % RET_C == 0 and batch % RET_BATCH == 0
    n_chunks = seq // RET_C
    n_pairs = RET_HEADS // 2
    kernel = functools.partial(_retention_kernel, n_chunks=n_chunks)
    qb, kb = OFF_RQ // LANES, OFF_RK // LANES
    vb = OFF_RV // (2 * RET_DV)
    return pl.pallas_call(
        kernel,
        out_shape=jax.ShapeDtypeStruct((batch, seq, RET_W), BF16),
        grid=(n_pairs, batch // RET_BATCH),
        in_specs=[
            pl.BlockSpec(memory_space=pltpu.SMEM),
            pl.BlockSpec((RET_BATCH, seq, LANES), lambda p, b: (b, 0, qb + p)),
            pl.BlockSpec((RET_BATCH, seq, LANES), lambda p, b: (b, 0, kb + p)),
            pl.BlockSpec((RET_BATCH, seq, 2 * RET_DV), lambda p, b: (b, 0, vb + p)),
            pl.BlockSpec((N_META, LANES), lambda p, b: (0, META_RK // LANES + p)),
            pl.BlockSpec((N_META, 2 * RET_DV), lambda p, b: (0, META_RV // (2 * RET_DV) + p)),
        ],
        out_specs=pl.BlockSpec((RET_BATCH, seq, 2 * RET_DV), lambda p, b: (b, 0, p)),
        scratch_shapes=[
            pltpu.VMEM((2, RET_C, LANES), F32),
            pltpu.VMEM((2, LANES, RET_C), F32),
            pltpu.VMEM((2, RET_C, RET_C), F32),
            pltpu.VMEM((2, RET_C, LANES), BF16),
            pltpu.VMEM((3, LANES, 2 * RET_DV), F32),
            pltpu.VMEM((N_META, LANES), F32),
            pltpu.VMEM((RET_BATCH, n_chunks, 2 * LANES, 2 * RET_DV), F32),
            pltpu.VMEM((RET_BATCH, n_chunks, 2 * LANES, 2 * RET_DV), BF16),
            pltpu.VMEM((RET_BATCH, n_chunks, RET_C, 2 * RET_DV), F32),
        ],
        compiler_params=pltpu.CompilerParams(
            dimension_semantics=("arbitrary", "arbitrary"), vmem_limit_bytes=VMEM_LIMIT),
        name="retention",
    )(log_gammas, proj, proj, proj, proj_meta, proj_meta)


def _pair_head_cols(w):
    w5 = w.reshape(w.shape[0], ATT_KV_HEADS // 2, 2, ATT_GROUP, ATT_DH)
    return w5.transpose(0, 1, 3, 2, 4).reshape(w.shape)


QB = 64
BAND = QB + 2 * WINDOW
META_PAD = 64
N_KEYS = BAND + META_PAD
ATT_ROWS = 2048
N_OFFSETS = 2 * WINDOW // QB + 1
SINK_COL = BAND + N_META
HB_UNROLL = 16


def _attention_kernel(sink_ref, q_ref, k_ref, v_ref, km_ref, vm_ref, *rest, seq, n_cast):
    cast_src, o_ref = rest[:n_cast], rest[n_cast]
    cast_dst, bias_ref = rest[n_cast + 1:2 * n_cast + 1], rest[2 * n_cast + 1]
    for src, dst in zip(cast_src, cast_dst):
        dst[...] = src[...].astype(BF16)
    _attention_body(sink_ref, q_ref, k_ref, v_ref, km_ref, vm_ref, o_ref, bias_ref, seq=seq)


def _attention_body(sink_ref, q_ref, k_ref, v_ref, km_ref, vm_ref, o_ref, bias_ref, *, seq):
    n = pl.program_id(1)

    @pl.when((pl.program_id(0) == 0) & (n == 0))
    def _build_bias():
        r = lax.broadcasted_iota(jnp.int32, (QB, N_KEYS), 0)
        j = lax.broadcasted_iota(jnp.int32, (QB, N_KEYS), 1)
        for t in range(N_OFFSETS):
            dist = jnp.abs(j - t * QB - r)
            band_ok = (j < BAND) & (dist <= WINDOW)
            ndist = dist.astype(F32) * (-LOG2E)
            for head in range(ATT_Q_HEADS):
                slope = 2.0 ** (-8.0 * (head + 1) / ATT_Q_HEADS)
                sink = sink_ref[head] * LOG2E
                rest = jnp.where((j >= BAND) & (j < SINK_COL), 0.0,
                                 jnp.where(j == SINK_COL, sink, NEG_INF))
                bias_ref[t, head] = jnp.where(band_ok, slope * ndist, rest)

    low_k = lax.broadcasted_iota(jnp.int32, (N_KEYS, LANES), 1) < ATT_DH

    def block_diag(x):
        zero = jnp.zeros_like(x)
        return jnp.concatenate([jnp.where(low_k, x, zero), jnp.where(low_k, zero, x)], axis=0)

    ones_bd = jnp.concatenate([jnp.where(low_k, 1.0, 0.0), jnp.where(low_k, 0.0, 1.0)],
                              axis=0).astype(BF16)
    meta_fill = jnp.zeros((META_PAD - N_META, LANES), BF16)

    def body(it, carry):
        for u in range(HB_UNROLL):
            hb = it * HB_UNROLL + u
            row0 = n * ATT_ROWS + hb * QB
            start = pl.multiple_of(jnp.clip(row0 - WINDOW, 0, seq - BAND), QB)
            t = (row0 - start) // QB
            rows = pl.ds(pl.multiple_of(hb * QB, QB), QB)
            for jp in range(ATT_KV_HEADS // 2):
                ksl = slice(jp * LANES, (jp + 1) * LANES)
                k_bd = block_diag(jnp.concatenate(
                    [k_ref[pl.ds(start, BAND), ksl], km_ref[:, ksl], meta_fill], axis=0))
                v_bd = jnp.concatenate([block_diag(jnp.concatenate(
                    [v_ref[pl.ds(start, BAND), ksl], vm_ref[:, ksl], meta_fill], axis=0)),
                    ones_bd], axis=1)
                groups = [slice((jp * ATT_GROUP + g) * LANES, (jp * ATT_GROUP + g + 1) * LANES)
                          for g in range(ATT_GROUP)]
                q_stack = jnp.concatenate([q_ref[rows, grp] for grp in groups], axis=0)
                s = _dot_nt(q_stack, k_bd)
                probs = []
                for g in range(ATT_GROUP):
                    p_halves = []
                    for half in range(2):
                        head = (2 * jp + half) * ATT_GROUP + g
                        sg = (s[g * QB:(g + 1) * QB, half * N_KEYS:(half + 1) * N_KEYS]
                              + bias_ref[t, head])
                        p = jnp.exp2(sg - jnp.max(sg, axis=-1, keepdims=True))
                        p_halves.append(p.astype(BF16))
                    probs.append(jnp.concatenate(p_halves, axis=1))
                o = _dot(jnp.concatenate(probs, axis=0), v_bd)
                out = o[:, 0:LANES] / o[:, LANES:2 * LANES]
                for g in range(ATT_GROUP):
                    o_ref[rows, groups[g]] = out[g * QB:(g + 1) * QB].astype(BF16)
        return carry

    lax.fori_loop(0, ATT_ROWS // QB // HB_UNROLL, body, 0)


def _paired_src_block(i, blocks_per_group):
    grp, sub = i // blocks_per_group, i % blocks_per_group
    j, g, half = grp // (2 * ATT_GROUP), (grp // 2) % ATT_GROUP, grp % 2
    return ((j * 2 + half) * ATT_GROUP + g) * blocks_per_group + sub


def _attention_call(sink, proj, proj_meta, w_casts, batch, seq):
    assert seq % ATT_ROWS == 0 and seq >= BAND
    n_blk = seq // ATT_ROWS
    steps = batch * n_blk
    kernel = functools.partial(_attention_kernel, seq=seq, n_cast=len(w_casts))
    kv_w = ATT_KV_HEADS * ATT_DH
    step = lambda b, n: b * n_blk + n
    cast_in, cast_out, cast_shapes = [], [], []
    for idx, w in enumerate(w_casts):
        rows, cols = w.shape
        reps = 1
        while rows % (steps // reps) or (rows // (steps // reps)) % 16:
            reps *= 2
        blk = rows // (steps // reps)
        dst_map = lambda b, n, reps=reps: (step(b, n) // reps, 0)
        if idx == 1:
            assert reps == 1 and ATT_DH % blk == 0
            src_map = lambda b, n, bpg=ATT_DH // blk: (_paired_src_block(step(b, n), bpg), 0)
        else:
            src_map = dst_map
        cast_in.append(pl.BlockSpec((blk, cols), src_map))
        cast_out.append(pl.BlockSpec((blk, cols), dst_map))
        cast_shapes.append(jax.ShapeDtypeStruct((rows, cols), BF16))
    return pl.pallas_call(
        kernel,
        out_shape=[jax.ShapeDtypeStruct((batch, seq, ATT_W), BF16)] + cast_shapes,
        grid=(batch, n_blk),
        in_specs=[
            pl.BlockSpec(memory_space=pltpu.SMEM),
            pl.BlockSpec((None, ATT_ROWS, ATT_W), lambda b, n: (b, n, OFF_AQ // ATT_W)),
            pl.BlockSpec((None, seq, kv_w), lambda b, n: (b, 0, OFF_AK // kv_w)),
            pl.BlockSpec((None, seq, kv_w), lambda b, n: (b, 0, OFF_AV // kv_w)),
            pl.BlockSpec((N_META, kv_w), lambda b, n: (0, META_AK // kv_w)),
            pl.BlockSpec((N_META, kv_w), lambda b, n: (0, META_AV // kv_w)),
        ] + cast_in,
        out_specs=[pl.BlockSpec((None, ATT_ROWS, ATT_W), lambda b, n: (b, n, 0))] + cast_out,
        scratch_shapes=[pltpu.VMEM((N_OFFSETS, ATT_Q_HEADS, QB, N_KEYS), F32)],
        compiler_params=pltpu.CompilerParams(
            dimension_semantics=("arbitrary", "arbitrary"), vmem_limit_bytes=VMEM_LIMIT),
        name="attention",
    )(sink, proj, proj, proj, proj_meta, proj_meta, *w_casts)


FF_CHUNK = 256
GATE_HALF = D_MODEL // 2


def _mixffn_kernel(x_ref, ret_ref, rg_ref, gain_ref, att_ref, ga0_ref, ga1_ref, gb0_ref, gb1_ref,
                   wr_ref, wa_ref, wo_ref, nf_ref, wg_ref, wu_ref, wd_ref, nfin_ref, o_ref,
                   r_ref, h_ref, act_ref):
    gb = jnp.concatenate([gb0_ref[...], gb1_ref[...]], axis=1).astype(F32)
    att_part = _sigmoid(gb) * _dot(att_ref[...], wa_ref[...])
    ret_part = None
    for h in range(RET_HEADS):
        sl = slice(h * RET_DV, (h + 1) * RET_DV)
        o = ret_ref[:, sl].astype(F32)
        d = o - jnp.mean(o, axis=-1, keepdims=True)
        var = jnp.mean(d * d, axis=-1, keepdims=True)
        g = rg_ref[:, sl].astype(F32)
        r_ref[:, sl] = (g * _sigmoid(g) * (d * lax.rsqrt(var + GN_EPS) * gain_ref[:, sl])).astype(BF16)
        if h % 2 == 1:
            ks = slice((h - 1) * RET_DV, (h + 1) * RET_DV)
            part = _dot(r_ref[:, ks], wr_ref[ks, :])
            ret_part = part if ret_part is None else ret_part + part
    ga = jnp.concatenate([ga0_ref[...], ga1_ref[...]], axis=1).astype(F32)
    merged = _sigmoid(ga) * ret_part + att_part
    x = x_ref[...] + _dot(merged.astype(BF16), wo_ref[...])
    ms = jnp.mean(x * x, axis=-1, keepdims=True)
    h_ref[...] = (x * lax.rsqrt(ms + RMS_EPS) * nf_ref[...]).astype(BF16)
    for c in range(0, D_FF, FF_CHUNK):
        a = _dot(h_ref[...], wg_ref[:, c:c + FF_CHUNK])
        b = _dot(h_ref[...], wu_ref[:, c:c + FF_CHUNK])
        act_ref[:, c:c + FF_CHUNK] = (a * _sigmoid(a) * b).astype(BF16)
    y = x + _dot(act_ref[...], wd_ref[...])
    ms = jnp.mean(y * y, axis=-1, keepdims=True)
    o_ref[...] = y * lax.rsqrt(ms + RMS_EPS) * nfin_ref[...]


def _mixffn_call(x2d, ret, gn_gain, att, proj, wr, wa, wo, norm_ffn, wgu, wd, norm_final, tm):
    rows = x2d.shape[0]
    const = lambda shape, col=0: pl.BlockSpec(shape, lambda i: (0, col),
                                              pipeline_mode=pl.Buffered(1))
    row_tile = lambda col: pl.BlockSpec((tm, D_MODEL), lambda i: (i, col))
    half_tile = lambda col: pl.BlockSpec((tm, GATE_HALF), lambda i: (i, col))
    return pl.pallas_call(
        _mixffn_kernel,
        out_shape=jax.ShapeDtypeStruct((rows, D_MODEL), F32),
        grid=(rows // tm,),
        in_specs=[
            row_tile(0), row_tile(0), row_tile(OFF_RG // RET_W), const((1, RET_W)), row_tile(0),
            half_tile(OFF_GA // GATE_HALF), half_tile(OFF_GA // GATE_HALF + 1),
            half_tile(OFF_GB // GATE_HALF), half_tile(OFF_GB // GATE_HALF + 1),
            const((RET_W, D_MODEL)), const((ATT_W, D_MODEL)), const((D_MODEL, D_MODEL)),
            const((1, D_MODEL)),
            const((D_MODEL, D_FF), 0), const((D_MODEL, D_FF), 1), const((D_FF, D_MODEL)),
            const((1, D_MODEL)),
        ],
        out_specs=row_tile(0),
        scratch_shapes=[pltpu.VMEM((tm, RET_W), BF16), pltpu.VMEM((tm, D_MODEL), BF16),
                        pltpu.VMEM((tm, D_FF), BF16)],
        compiler_params=pltpu.CompilerParams(
            dimension_semantics=("parallel",), vmem_limit_bytes=VMEM_LIMIT),
        name="mixffn",
    )(x2d, ret, proj, gn_gain, att, proj, proj, proj, proj, wr, wa, wo, norm_ffn, wgu, wgu, wd,
      norm_final)


def kernel(x, meta_tokens, w_in, ret_decay_logit_fwd, ret_decay_logit_bwd, ret_gn_gain, attn_sink,
           w_branch_ret, w_branch_att, w_out, norm_mix, norm_ffn, w_gate_up, w_down, norm_final):
    batch, seq, d = x.shape
    assert d == D_MODEL and w_in.shape[0] == 1

    w_in0 = w_in[0]
    w_in_b = w_in0.astype(BF16)
    w_q_b = _pair_head_cols(w_in0[:, OFF_AQ:OFF_AK]).astype(BF16)
    nmix = norm_mix[0].reshape(1, D_MODEL)
    nffn = norm_ffn[0].reshape(1, D_MODEL)
    nfin = norm_final.reshape(1, D_MODEL)

    x2d = x.reshape(batch * seq, D_MODEL)
    proj = _inproj_call(x2d, nmix, w_in_b, w_q_b, INPROJ_ROWS)
    proj_meta = _meta_call(meta_tokens.astype(x.dtype), nmix, w_in_b)
    proj3 = proj.reshape(batch, seq, IN_COLS)

    log_gammas = jnp.stack([jax.nn.log_sigmoid(ret_decay_logit_fwd[0].astype(F32)),
                            jax.nn.log_sigmoid(ret_decay_logit_bwd[0].astype(F32))])
    ret = _retention_call(log_gammas, proj3, proj_meta, batch, seq)

    att, wr, wa, wo, wgu, wd = _attention_call(
        attn_sink[0].astype(F32), proj3, proj_meta,
        (w_branch_ret[0], w_branch_att[0], w_out[0], w_gate_up[0], w_down[0]), batch, seq)

    out = _mixffn_call(x2d, ret.reshape(batch * seq, RET_W), ret_gn_gain[0].reshape(1, RET_W),
                       att.reshape(batch * seq, ATT_W), proj,
                       wr, wa, wo, nffn, wgu, wd, nfin, MIXFFN_ROWS)
    return out.reshape(batch, seq, D_MODEL)
```

```python
import functools

import numpy as np
import jax
import jax.numpy as jnp
from jax import lax
from jax.experimental import pallas as pl
from jax.experimental.pallas import tpu as pltpu

D_MODEL = 1024
N_META = 16
RET_HEADS = 8
RET_DK = 64
RET_DV = 128
RET_W = RET_HEADS * RET_DV
ATT_Q_HEADS = 16
ATT_KV_HEADS = 4
ATT_GROUP = ATT_Q_HEADS // ATT_KV_HEADS
ATT_DH = 64
ATT_W = ATT_Q_HEADS * ATT_DH
WINDOW = 128
D_FF = 2816
IN_SIZES = (512, 512, 1024, 1024, 1024, 256, 256, 1024, 1024)
IN_COLS = sum(IN_SIZES)
OFF_RQ, OFF_RK, OFF_RV, OFF_RG, OFF_AQ, OFF_AK, OFF_AV, OFF_GA, OFF_GB = (
    int(c) for c in np.cumsum((0,) + IN_SIZES[:-1]))
RMS_EPS = 1e-6
GN_EPS = 1e-5
NEG_INF = -1e30
LOG2E = 1.4426950408889634
ATT_Q_SCALE = ATT_DH ** -0.5 * LOG2E

LANES = 128
VMEM_LIMIT = 56 * 1024 * 1024
INPROJ_ROWS = 1024
MIXFFN_ROWS = 512

F32 = jnp.float32
BF16 = jnp.bfloat16


def _sigmoid(x):
    return 1.0 / (1.0 + jnp.exp(-x))


def _dot(a, b):
    return jnp.dot(a, b, preferred_element_type=F32)


def _dot_nt(a, b):
    return lax.dot_general(a, b, (((1,), (1,)), ((), ())), preferred_element_type=F32)


def _dot_tn(a, b):
    return lax.dot_general(a, b, (((0,), (0,)), ((), ())), preferred_element_type=F32)


IN_CHUNK = 512
IN_SUB = 512


def _inproj_kernel(x_ref, g_ref, w_ref, wq_ref, o_ref, h_ref):
    tm = x_ref.shape[0]
    sub = min(IN_SUB, tm)
    for r0 in range(0, tm, sub):
        rows = slice(r0, r0 + sub)
        x = x_ref[rows, :]
        ms = jnp.mean(x * x, axis=-1, keepdims=True)
        h_ref[rows, :] = (x * lax.rsqrt(ms + RMS_EPS) * g_ref[...]).astype(BF16)
        for c in range(0, IN_COLS, IN_CHUNK):
            if OFF_AQ <= c < OFF_AQ + ATT_W:
                q0 = c - OFF_AQ
                acc = _dot(h_ref[rows, :], wq_ref[:, q0:q0 + IN_CHUNK]) * ATT_Q_SCALE
            else:
                acc = _dot(h_ref[rows, :], w_ref[:, c:c + IN_CHUNK])
            o_ref[rows, c:c + IN_CHUNK] = acc.astype(BF16)


def _inproj_call(x2d, gain, w_bf16, wq_bf16, tm):
    rows = x2d.shape[0]
    const = lambda shape: pl.BlockSpec(shape, lambda i: (0, 0), pipeline_mode=pl.Buffered(1))
    return pl.pallas_call(
        _inproj_kernel,
        out_shape=jax.ShapeDtypeStruct((rows, IN_COLS), BF16),
        grid=(rows // tm,),
        in_specs=[
            pl.BlockSpec((tm, D_MODEL), lambda i: (i, 0)),
            pl.BlockSpec((1, D_MODEL), lambda i: (0, 0)),
            const((D_MODEL, IN_COLS)), const((D_MODEL, ATT_W)),
        ],
        out_specs=pl.BlockSpec((tm, IN_COLS), lambda i: (i, 0)),
        scratch_shapes=[pltpu.VMEM((tm, D_MODEL), BF16)],
        compiler_params=pltpu.CompilerParams(
            dimension_semantics=("parallel",), vmem_limit_bytes=VMEM_LIMIT),
        name="inproj",
    )(x2d, gain, w_bf16, wq_bf16)


META_RK = 0
META_RV = META_RK + RET_HEADS * RET_DK
META_AK = META_RV + RET_W
META_AV = META_AK + ATT_KV_HEADS * ATT_DH
META_COLS = META_AV + ATT_KV_HEADS * ATT_DH


def _meta_kernel(x_ref, g_ref, wk_ref, wv_ref, wa_ref, o_ref):
    x = x_ref[...]
    ms = jnp.mean(x * x, axis=-1, keepdims=True)
    h = (x * lax.rsqrt(ms + RMS_EPS) * g_ref[...]).astype(BF16)
    o_ref[:, META_RK:META_RV] = _dot(h, wk_ref[...]).astype(BF16)
    o_ref[:, META_RV:META_AK] = _dot(h, wv_ref[...]).astype(BF16)
    o_ref[:, META_AK:META_COLS] = _dot(h, wa_ref[...]).astype(BF16)


def _meta_call(meta, gain, w_bf16):
    kv_w = ATT_KV_HEADS * ATT_DH
    assert OFF_AV == OFF_AK + kv_w
    piece = lambda width, off: pl.BlockSpec((D_MODEL, width), lambda i: (0, off // width))
    return pl.pallas_call(
        _meta_kernel,
        out_shape=jax.ShapeDtypeStruct((N_META, META_COLS), BF16),
        grid=(1,),
        in_specs=[
            pl.BlockSpec((N_META, D_MODEL), lambda i: (0, 0)),
            pl.BlockSpec((1, D_MODEL), lambda i: (0, 0)),
            piece(RET_HEADS * RET_DK, OFF_RK), piece(RET_W, OFF_RV), piece(2 * kv_w, OFF_AK),
        ],
        out_specs=pl.BlockSpec((N_META, META_COLS), lambda i: (0, 0)),
        name="metaproj",
    )(meta, gain, w_bf16, w_bf16, w_bf16)


RET_C = 256
RET_BATCH = 4


def _retention_kernel(lg_ref, q_ref, k_ref, v_ref, km_ref, vm_ref, o_ref,
                      tq_ref, tk_ref, td_ref, sel_ref, ts_ref, tm_ref, u_ref, s_ref, *, n_chunks):
    pair = pl.program_id(0)
    C = RET_C
    k_scale = RET_DK ** -0.5

    @pl.when(pl.program_id(1) == 0)
    def _build_tables():
        lgf = [lg_ref[0, 2 * pair + hl] for hl in range(2)]
        lgb = [lg_ref[1, 2 * pair + hl] for hl in range(2)]
        row = lax.broadcasted_iota(jnp.int32, (C, LANES), 0).astype(F32)
        first = lax.broadcasted_iota(jnp.int32, (C, LANES), 1) < RET_DK
        pos_t = lax.broadcasted_iota(jnp.int32, (LANES, C), 1).astype(F32)
        first_t = lax.broadcasted_iota(jnp.int32, (LANES, C), 0) < RET_DK
        diff = (lax.broadcasted_iota(jnp.int32, (C, C), 0)
                - lax.broadcasted_iota(jnp.int32, (C, C), 1)).astype(F32)
        srow = lax.broadcasted_iota(jnp.int32, (LANES, 2 * RET_DV), 0)
        scol = lax.broadcasted_iota(jnp.int32, (LANES, 2 * RET_DV), 1)
        mrow = lax.broadcasted_iota(jnp.int32, (N_META, LANES), 0).astype(F32)
        mfirst = lax.broadcasted_iota(jnp.int32, (N_META, LANES), 1) < RET_DK

        def both(sel, fn, lg):
            return jnp.where(sel, fn(lg[0]), fn(lg[1]))

        tq_ref[0] = both(first, lambda l: jnp.exp((row + 1.0) * l), lgf)
        tq_ref[1] = both(first, lambda l: jnp.exp((C - row) * l), lgb)
        tk_ref[0] = both(first_t, lambda l: jnp.exp((C - 1.0 - pos_t) * l), lgf) * k_scale
        tk_ref[1] = both(first_t, lambda l: jnp.exp(pos_t * l), lgb) * k_scale
        tm_ref[...] = both(mfirst, lambda l: jnp.exp((N_META - 1.0 - mrow) * l), lgf) * k_scale
        for hl in range(2):
            td_ref[hl] = k_scale * jnp.where(diff <= 0, jnp.exp(jnp.maximum(-diff, 0.0) * lgf[hl]),
                                             jnp.exp(jnp.maximum(diff, 0.0) * lgb[hl]))
            sel_ref[hl] = jnp.where(first == (hl == 0), 1.0, 0.0).astype(BF16)
        own = (srow < RET_DK) == (scol < RET_DV)
        zeros = jnp.zeros((LANES, 2 * RET_DV), F32)
        ts_ref[0] = jnp.where(own, both(srow < RET_DK, lambda l: jnp.exp(zeros + C * l), lgf), 0.0)
        ts_ref[1] = jnp.where(own, both(srow < RET_DK, lambda l: jnp.exp(zeros + C * l), lgb), 0.0)
        ts_ref[2] = jnp.where(own, 1.0, 0.0)

    def head(x, hl):
        return x[:, hl * RET_DV:(hl + 1) * RET_DV]

    own_mask = ts_ref[2]
    for bb in range(RET_BATCH):
        _retention_one(q_ref.at[bb], k_ref.at[bb], v_ref.at[bb], km_ref, vm_ref, o_ref.at[bb],
                       tq_ref, tk_ref, td_ref, sel_ref, ts_ref, tm_ref, u_ref.at[bb], s_ref.at[bb],
                       own_mask, head, n_chunks)


def _retention_one(q_ref, k_ref, v_ref, km_ref, vm_ref, o_ref, tq_ref, tk_ref, td_ref, sel_ref,
                   ts_ref, tm_ref, u_ref, s_ref, own_mask, head, n_chunks):
    C = RET_C

    for c in range(n_chunks):
        rows = slice(c * C, (c + 1) * C)
        k_t = k_ref[rows, :].astype(F32).T
        lhs = jnp.concatenate([k_t * tk_ref[0], k_t * tk_ref[1]], axis=0).astype(BF16)
        u = _dot(lhs, v_ref[rows, :])
        u_ref[c, 0:LANES, :] = u[0:LANES] * own_mask
        u_ref[c, LANES:2 * LANES, :] = u[LANES:2 * LANES] * own_mask

    sf = _dot_tn((km_ref[...].astype(F32) * tm_ref[...]).astype(BF16), vm_ref[...]) * own_mask
    for c in range(n_chunks):
        s_ref[c, 0:LANES, :] = sf.astype(BF16)
        if c + 1 < n_chunks:
            sf = sf * ts_ref[0] + u_ref[c, 0:LANES, :]
    sb = jnp.zeros((LANES, 2 * RET_DV), F32)
    for c in reversed(range(n_chunks)):
        s_ref[c, LANES:2 * LANES, :] = sb.astype(BF16)
        if c > 0:
            sb = sb * ts_ref[1] + u_ref[c, LANES:2 * LANES, :]

    for c in range(n_chunks):
        rows = slice(c * C, (c + 1) * C)
        q16 = q_ref[rows, :]
        kc = k_ref[rows, :]
        vc = v_ref[rows, :]
        qc = q16.astype(F32)
        q_cross = jnp.concatenate([(qc * tq_ref[0]).astype(BF16),
                                   (qc * tq_ref[1]).astype(BF16)], axis=1)
        cross = _dot(q_cross, s_ref[c])
        for hl in range(2):
            s_t = _dot_nt(kc, q16 * sel_ref[hl]) * td_ref[hl]
            o_t = _dot_tn(head(vc, hl), s_t.astype(BF16))
            o_ref[rows, hl * RET_DV:(hl + 1) * RET_DV] = (head(cross, hl) + o_t.T).astype(BF16)


def _retention_call(log_gammas, proj, proj_meta, batch, seq):
    assert seq % RET_C == 0 and batch % RET_BATCH == 0
    n_chunks = seq // RET_C
    n_pairs = RET_HEADS // 2
    kernel = functools.partial(_retention_kernel, n_chunks=n_chunks)
    qb, kb = OFF_RQ // LANES, OFF_RK // LANES
    vb = OFF_RV // (2 * RET_DV)
    return pl.pallas_call(
        kernel,
        out_shape=jax.ShapeDtypeStruct((batch, seq, RET_W), BF16),
        grid=(n_pairs, batch // RET_BATCH),
        in_specs=[
            pl.BlockSpec(memory_space=pltpu.SMEM),
            pl.BlockSpec((RET_BATCH, seq, LANES), lambda p, b: (b, 0, qb + p)),
            pl.BlockSpec((RET_BATCH, seq, LANES), lambda p, b: (b, 0, kb + p)),
            pl.BlockSpec((RET_BATCH, seq, 2 * RET_DV), lambda p, b: (b, 0, vb + p)),
            pl.BlockSpec((N_META, LANES), lambda p, b: (0, META_RK // LANES + p)),
            pl.BlockSpec((N_META, 2 * RET_DV), lambda p, b: (0, META_RV // (2 * RET_DV) + p)),
        ],
        out_specs=pl.BlockSpec((RET_BATCH, seq, 2 * RET_DV), lambda p, b: (b, 0, p)),
        scratch_shapes=[
            pltpu.VMEM((2, RET_C, LANES), F32),
            pltpu.VMEM((2, LANES, RET_C), F32),
            pltpu.VMEM((2, RET_C, RET_C), F32),
            pltpu.VMEM((2, RET_C, LANES), BF16),
            pltpu.VMEM((3, LANES, 2 * RET_DV), F32),
            pltpu.VMEM((N_META, LANES), F32),
            pltpu.VMEM((RET_BATCH, n_chunks, 2 * LANES, 2 * RET_DV), F32),
            pltpu.VMEM((RET_BATCH, n_chunks, 2 * LANES, 2 * RET_DV), BF16),
        ],
        compiler_params=pltpu.CompilerParams(
            dimension_semantics=("arbitrary", "arbitrary"), vmem_limit_bytes=VMEM_LIMIT),
        name="retention",
    )(log_gammas, proj, proj, proj, proj_meta, proj_meta)


def _pair_head_cols(w):
    w5 = w.reshape(w.shape[0], ATT_KV_HEADS // 2, 2, ATT_GROUP, ATT_DH)
    return w5.transpose(0, 1, 3, 2, 4).reshape(w.shape)


QB = 64
BAND = QB + 2 * WINDOW
META_PAD = 64
N_KEYS = BAND + META_PAD
ATT_ROWS = 2048
N_OFFSETS = 2 * WINDOW // QB + 1
SINK_COL = BAND + N_META
HB_UNROLL = 16


def _attention_kernel(sink_ref, q_ref, k_ref, v_ref, km_ref, vm_ref, *rest, seq, n_cast):
    cast_src, o_ref = rest[:n_cast], rest[n_cast]
    cast_dst, bias_ref = rest[n_cast + 1:2 * n_cast + 1], rest[2 * n_cast + 1]
    for src, dst in zip(cast_src, cast_dst):
        dst[...] = src[...].astype(BF16)
    _attention_body(sink_ref, q_ref, k_ref, v_ref, km_ref, vm_ref, o_ref, bias_ref, seq=seq)


def _attention_body(sink_ref, q_ref, k_ref, v_ref, km_ref, vm_ref, o_ref, bias_ref, *, seq):
    n = pl.program_id(1)

    @pl.when((pl.program_id(0) == 0) & (n == 0))
    def _build_bias():
        r = lax.broadcasted_iota(jnp.int32, (QB, N_KEYS), 0)
        j = lax.broadcasted_iota(jnp.int32, (QB, N_KEYS), 1)
        for t in range(N_OFFSETS):
            dist = jnp.abs(j - t * QB - r)
            band_ok = (j < BAND) & (dist <= WINDOW)
            ndist = dist.astype(F32) * (-LOG2E)
            for head in range(ATT_Q_HEADS):
                slope = 2.0 ** (-8.0 * (head + 1) / ATT_Q_HEADS)
                sink = sink_ref[head] * LOG2E
                rest = jnp.where((j >= BAND) & (j < SINK_COL), 0.0,
                                 jnp.where(j == SINK_COL, sink, NEG_INF))
                bias_ref[t, head] = jnp.where(band_ok, slope * ndist, rest)

    low_k = lax.broadcasted_iota(jnp.int32, (N_KEYS, LANES), 1) < ATT_DH

    def block_diag(x):
        zero = jnp.zeros_like(x)
        return jnp.concatenate([jnp.where(low_k, x, zero), jnp.where(low_k, zero, x)], axis=0)

    ones_bd = jnp.concatenate([jnp.where(low_k, 1.0, 0.0), jnp.where(low_k, 0.0, 1.0)],
                              axis=0).astype(BF16)
    meta_fill = jnp.zeros((META_PAD - N_META, LANES), BF16)

    def body(it, carry):
        for u in range(HB_UNROLL):
            hb = it * HB_UNROLL + u
            row0 = n * ATT_ROWS + hb * QB
            start = pl.multiple_of(jnp.clip(row0 - WINDOW, 0, seq - BAND), QB)
            t = (row0 - start) // QB
            rows = pl.ds(pl.multiple_of(hb * QB, QB), QB)
            for jp in range(ATT_KV_HEADS // 2):
                ksl = slice(jp * LANES, (jp + 1) * LANES)
                k_bd = block_diag(jnp.concatenate(
                    [k_ref[pl.ds(start, BAND), ksl], km_ref[:, ksl], meta_fill], axis=0))
                v_bd = jnp.concatenate([block_diag(jnp.concatenate(
                    [v_ref[pl.ds(start, BAND), ksl], vm_ref[:, ksl], meta_fill], axis=0)),
                    ones_bd], axis=1)
                groups = [slice((jp * ATT_GROUP + g) * LANES, (jp * ATT_GROUP + g + 1) * LANES)
                          for g in range(ATT_GROUP)]
                q_stack = jnp.concatenate([q_ref[rows, grp] for grp in groups], axis=0)
                s = _dot_nt(q_stack, k_bd)
                probs = []
                for g in range(ATT_GROUP):
                    p_halves = []
                    for half in range(2):
                        head = (2 * jp + half) * ATT_GROUP + g
                        sg = (s[g * QB:(g + 1) * QB, half * N_KEYS:(half + 1) * N_KEYS]
                              + bias_ref[t, head])
                        p = jnp.exp2(sg - jnp.max(sg, axis=-1, keepdims=True))
                        p_halves.append(p.astype(BF16))
                    probs.append(jnp.concatenate(p_halves, axis=1))
                o = _dot(jnp.concatenate(probs, axis=0), v_bd)
                out = o[:, 0:LANES] / o[:, LANES:2 * LANES]
                for g in range(ATT_GROUP):
                    o_ref[rows, groups[g]] = out[g * QB:(g + 1) * QB].astype(BF16)
        return carry

    lax.fori_loop(0, ATT_ROWS // QB // HB_UNROLL, body, 0)


def _paired_src_block(i, blocks_per_group):
    grp, sub = i // blocks_per_group, i % blocks_per_group
    j, g, half = grp // (2 * ATT_GROUP), (grp // 2) % ATT_GROUP, grp % 2
    return ((j * 2 + half) * ATT_GROUP + g) * blocks_per_group + sub


def _attention_call(sink, proj, proj_meta, w_casts, batch, seq):
    assert seq % ATT_ROWS == 0 and seq >= BAND
    n_blk = seq // ATT_ROWS
    steps = batch * n_blk
    kernel = functools.partial(_attention_kernel, seq=seq, n_cast=len(w_casts))
    kv_w = ATT_KV_HEADS * ATT_DH
    step = lambda b, n: b * n_blk + n
    cast_in, cast_out, cast_shapes = [], [], []
    for idx, w in enumerate(w_casts):
        rows, cols = w.shape
        reps = 1
        while rows % (steps // reps) or (rows // (steps // reps)) % 16:
            reps *= 2
        blk = rows // (steps // reps)
        dst_map = lambda b, n, reps=reps: (step(b, n) // reps, 0)
        if idx == 1:
            assert reps == 1 and ATT_DH % blk == 0
            src_map = lambda b, n, bpg=ATT_DH // blk: (_paired_src_block(step(b, n), bpg), 0)
        else:
            src_map = dst_map
        cast_in.append(pl.BlockSpec((blk, cols), src_map))
        cast_out.append(pl.BlockSpec((blk, cols), dst_map))
        cast_shapes.append(jax.ShapeDtypeStruct((rows, cols), BF16))
    return pl.pallas_call(
        kernel,
        out_shape=[jax.ShapeDtypeStruct((batch, seq, ATT_W), BF16)] + cast_shapes,
        grid=(batch, n_blk),
        in_specs=[
            pl.BlockSpec(memory_space=pltpu.SMEM),
            pl.BlockSpec((None, ATT_ROWS, ATT_W), lambda b, n: (b, n, OFF_AQ // ATT_W)),
            pl.BlockSpec((None, seq, kv_w), lambda b, n: (b, 0, OFF_AK // kv_w)),
            pl.BlockSpec((None, seq, kv_w), lambda b, n: (b, 0, OFF_AV // kv_w)),
            pl.BlockSpec((N_META, kv_w), lambda b, n: (0, META_AK // kv_w)),
            pl.BlockSpec((N_META, kv_w), lambda b, n: (0, META_AV // kv_w)),
        ] + cast_in,
        out_specs=[pl.BlockSpec((None, ATT_ROWS, ATT_W), lambda b, n: (b, n, 0))] + cast_out,
        scratch_shapes=[pltpu.VMEM((N_OFFSETS, ATT_Q_HEADS, QB, N_KEYS), F32)],
        compiler_params=pltpu.CompilerParams(
            dimension_semantics=("arbitrary", "arbitrary"), vmem_limit_bytes=VMEM_LIMIT),
        name="attention",
    )(sink, proj, proj, proj, proj_meta, proj_meta, *w_casts)


FF_CHUNK = 256
GATE_HALF = D_MODEL // 2


def _mixffn_kernel(x_ref, ret_ref, rg_ref, gain_ref, att_ref, ga0_ref, ga1_ref, gb0_ref, gb1_ref,
                   wr_ref, wa_ref, wo_ref, nf_ref, wg_ref, wu_ref, wd_ref, nfin_ref, o_ref,
                   r_ref, h_ref, act_ref):
    gb = jnp.concatenate([gb0_ref[...], gb1_ref[...]], axis=1).astype(F32)
    att_part = _sigmoid(gb) * _dot(att_ref[...], wa_ref[...])
    ret_part = None
    for h in range(RET_HEADS):
        sl = slice(h * RET_DV, (h + 1) * RET_DV)
        o = ret_ref[:, sl].astype(F32)
        d = o - jnp.mean(o, axis=-1, keepdims=True)
        var = jnp.mean(d * d, axis=-1, keepdims=True)
        g = rg_ref[:, sl].astype(F32)
        r_ref[:, sl] = (g * _sigmoid(g) * (d * lax.rsqrt(var + GN_EPS) * gain_ref[:, sl])).astype(BF16)
        if h % 2 == 1:
            ks = slice((h - 1) * RET_DV, (h + 1) * RET_DV)
            part = _dot(r_ref[:, ks], wr_ref[ks, :])
            ret_part = part if ret_part is None else ret_part + part
    ga = jnp.concatenate([ga0_ref[...], ga1_ref[...]], axis=1).astype(F32)
    merged = _sigmoid(ga) * ret_part + att_part
    x = x_ref[...] + _dot(merged.astype(BF16), wo_ref[...])
    ms = jnp.mean(x * x, axis=-1, keepdims=True)
    h_ref[...] = (x * lax.rsqrt(ms + RMS_EPS) * nf_ref[...]).astype(BF16)
    for c in range(0, D_FF, FF_CHUNK):
        a = _dot(h_ref[...], wg_ref[:, c:c + FF_CHUNK])
        b = _dot(h_ref[...], wu_ref[:, c:c + FF_CHUNK])
        act_ref[:, c:c + FF_CHUNK] = (a * _sigmoid(a) * b).astype(BF16)
    y = x + _dot(act_ref[...], wd_ref[...])
    ms = jnp.mean(y * y, axis=-1, keepdims=True)
    o_ref[...] = y * lax.rsqrt(ms + RMS_EPS) * nfin_ref[...]


def _mixffn_call(x2d, ret, gn_gain, att, proj, wr, wa, wo, norm_ffn, wgu, wd, norm_final, tm):
    rows = x2d.shape[0]
    const = lambda shape, col=0: pl.BlockSpec(shape, lambda i: (0, col),
                                              pipeline_mode=pl.Buffered(1))
    row_tile = lambda col: pl.BlockSpec((tm, D_MODEL), lambda i: (i, col))
    half_tile = lambda col: pl.BlockSpec((tm, GATE_HALF), lambda i: (i, col))
    return pl.pallas_call(
        _mixffn_kernel,
        out_shape=jax.ShapeDtypeStruct((rows, D_MODEL), F32),
        grid=(rows // tm,),
        in_specs=[
            row_tile(0), row_tile(0), row_tile(OFF_RG // RET_W), const((1, RET_W)), row_tile(0),
            half_tile(OFF_GA // GATE_HALF), half_tile(OFF_GA // GATE_HALF + 1),
            half_tile(OFF_GB // GATE_HALF), half_tile(OFF_GB // GATE_HALF + 1),
            const((RET_W, D_MODEL)), const((ATT_W, D_MODEL)), const((D_MODEL, D_MODEL)),
            const((1, D_MODEL)),
            const((D_MODEL, D_FF), 0), const((D_MODEL, D_FF), 1), const((D_FF, D_MODEL)),
            const((1, D_MODEL)),
        ],
        out_specs=row_tile(0),
        scratch_shapes=[pltpu.VMEM((tm, RET_W), BF16), pltpu.VMEM((tm, D_MODEL), BF16),
                        pltpu.VMEM((tm, D_FF), BF16)],
        compiler_params=pltpu.CompilerParams(
            dimension_semantics=("parallel",), vmem_limit_bytes=VMEM_LIMIT),
        name="mixffn",
    )(x2d, ret, proj, gn_gain, att, proj, proj, proj, proj, wr, wa, wo, norm_ffn, wgu, wgu, wd,
      norm_final)


def kernel(x, meta_tokens, w_in, ret_decay_logit_fwd, ret_decay_logit_bwd, ret_gn_gain, attn_sink,
           w_branch_ret, w_branch_att, w_out, norm_mix, norm_ffn, w_gate_up, w_down, norm_final):
    batch, seq, d = x.shape
    assert d == D_MODEL and w_in.shape[0] == 1

    w_in0 = w_in[0]
    w_in_b = w_in0.astype(BF16)
    w_q_b = _pair_head_cols(w_in0[:, OFF_AQ:OFF_AK]).astype(BF16)
    nmix = norm_mix[0].reshape(1, D_MODEL)
    nffn = norm_ffn[0].reshape(1, D_MODEL)
    nfin = norm_final.reshape(1, D_MODEL)

    x2d = x.reshape(batch * seq, D_MODEL)
    proj = _inproj_call(x2d, nmix, w_in_b, w_q_b, INPROJ_ROWS)
    proj_meta = _meta_call(meta_tokens.astype(x.dtype), nmix, w_in_b)
    proj3 = proj.reshape(batch, seq, IN_COLS)

    log_gammas = jnp.stack([jax.nn.log_sigmoid(ret_decay_logit_fwd[0].astype(F32)),
                            jax.nn.log_sigmoid(ret_decay_logit_bwd[0].astype(F32))])
    ret = _retention_call(log_gammas, proj3, proj_meta, batch, seq)

    att, wr, wa, wo, wgu, wd = _attention_call(
        attn_sink[0].astype(F32), proj3, proj_meta,
        (w_branch_ret[0], w_branch_att[0], w_out[0], w_gate_up[0], w_down[0]), batch, seq)

    out = _mixffn_call(x2d, ret.reshape(batch * seq, RET_W), ret_gn_gain[0].reshape(1, RET_W),
                       att.reshape(batch * seq, ATT_W), proj,
                       wr, wa, wo, nffn, wgu, wd, nfin, MIXFFN_ROWS)
    return out.reshape(batch, seq, D_MODEL)
```

```python
import functools

import numpy as np
import jax
import jax.numpy as jnp
from jax import lax
from jax.experimental import pallas as pl
from jax.experimental.pallas import tpu as pltpu

D_MODEL = 1024
N_META = 16
RET_HEADS = 8
RET_DK = 64
RET_DV = 128
RET_W = RET_HEADS * RET_DV
ATT_Q_HEADS = 16
ATT_KV_HEADS = 4
ATT_GROUP = ATT_Q_HEADS // ATT_KV_HEADS
ATT_DH = 64
ATT_W = ATT_Q_HEADS * ATT_DH
WINDOW = 128
D_FF = 2816
IN_SIZES = (512, 512, 1024, 1024, 1024, 256, 256, 1024, 1024)
IN_COLS = sum(IN_SIZES)
OFF_RQ, OFF_RK, OFF_RV, OFF_RG, OFF_AQ, OFF_AK, OFF_AV, OFF_GA, OFF_GB = (
    int(c) for c in np.cumsum((0,) + IN_SIZES[:-1]))
RMS_EPS = 1e-6
GN_EPS = 1e-5
NEG_INF = -1e30
LOG2E = 1.4426950408889634
ATT_Q_SCALE = ATT_DH ** -0.5 * LOG2E

LANES = 128
VMEM_LIMIT = 56 * 1024 * 1024
INPROJ_ROWS = 1024
MIXFFN_ROWS = 512

F32 = jnp.float32
BF16 = jnp.bfloat16


def _sigmoid(x):
    return 1.0 / (1.0 + jnp.exp(-x))


def _dot(a, b):
    return jnp.dot(a, b, preferred_element_type=F32)


def _dot_nt(a, b):
    return lax.dot_general(a, b, (((1,), (1,)), ((), ())), preferred_element_type=F32)


def _dot_tn(a, b):
    return lax.dot_general(a, b, (((0,), (0,)), ((), ())), preferred_element_type=F32)


IN_CHUNK = 512
IN_SUB = 256


def _inproj_kernel(x_ref, g_ref, w_ref, wq_ref, o_ref, h_ref):
    tm = x_ref.shape[0]
    sub = min(IN_SUB, tm)
    for r0 in range(0, tm, sub):
        rows = slice(r0, r0 + sub)
        x = x_ref[rows, :]
        ms = jnp.mean(x * x, axis=-1, keepdims=True)
        h_ref[rows, :] = (x * lax.rsqrt(ms + RMS_EPS) * g_ref[...]).astype(BF16)
        for c in range(0, IN_COLS, IN_CHUNK):
            if OFF_AQ <= c < OFF_AQ + ATT_W:
                q0 = c - OFF_AQ
                acc = _dot(h_ref[rows, :], wq_ref[:, q0:q0 + IN_CHUNK]) * ATT_Q_SCALE
            else:
                acc = _dot(h_ref[rows, :], w_ref[:, c:c + IN_CHUNK])
            o_ref[rows, c:c + IN_CHUNK] = acc.astype(BF16)


def _inproj_call(x2d, gain, w_bf16, wq_bf16, tm):
    rows = x2d.shape[0]
    const = lambda shape: pl.BlockSpec(shape, lambda i: (0, 0), pipeline_mode=pl.Buffered(1))
    return pl.pallas_call(
        _inproj_kernel,
        out_shape=jax.ShapeDtypeStruct((rows, IN_COLS), BF16),
        grid=(rows // tm,),
        in_specs=[
            pl.BlockSpec((tm, D_MODEL), lambda i: (i, 0)),
            pl.BlockSpec((1, D_MODEL), lambda i: (0, 0)),
            const((D_MODEL, IN_COLS)), const((D_MODEL, ATT_W)),
        ],
        out_specs=pl.BlockSpec((tm, IN_COLS), lambda i: (i, 0)),
        scratch_shapes=[pltpu.VMEM((tm, D_MODEL), BF16)],
        compiler_params=pltpu.CompilerParams(
            dimension_semantics=("parallel",), vmem_limit_bytes=VMEM_LIMIT),
        name="inproj",
    )(x2d, gain, w_bf16, wq_bf16)


MTOK_RK = 0
MTOK_RV = MTOK_RK + RET_HEADS * RET_DK
MTOK_AK = MTOK_RV + RET_W
MTOK_AV = MTOK_AK + ATT_KV_HEADS * ATT_DH
MTOK_COLS = MTOK_AV + ATT_KV_HEADS * ATT_DH


def _meta_kernel(x_ref, g_ref, wk_ref, wv_ref, wa_ref, o_ref):
    x = x_ref[...]
    ms = jnp.mean(x * x, axis=-1, keepdims=True)
    h = (x * lax.rsqrt(ms + RMS_EPS) * g_ref[...]).astype(BF16)
    o_ref[:, MTOK_RK:MTOK_RV] = _dot(h, wk_ref[...]).astype(BF16)
    o_ref[:, MTOK_RV:MTOK_AK] = _dot(h, wv_ref[...]).astype(BF16)
    o_ref[:, MTOK_AK:MTOK_COLS] = _dot(h, wa_ref[...]).astype(BF16)


def _meta_call(meta, gain, w_bf16):
    kv_w = ATT_KV_HEADS * ATT_DH
    assert OFF_AV == OFF_AK + kv_w
    piece = lambda width, off: pl.BlockSpec((D_MODEL, width), lambda i: (0, off // width))
    return pl.pallas_call(
        _meta_kernel,
        out_shape=jax.ShapeDtypeStruct((N_META, MTOK_COLS), BF16),
        grid=(1,),
        in_specs=[
            pl.BlockSpec((N_META, D_MODEL), lambda i: (0, 0)),
            pl.BlockSpec((1, D_MODEL), lambda i: (0, 0)),
            piece(RET_HEADS * RET_DK, OFF_RK), piece(RET_W, OFF_RV), piece(2 * kv_w, OFF_AK),
        ],
        out_specs=pl.BlockSpec((N_META, MTOK_COLS), lambda i: (0, 0)),
        name="metaproj",
    )(meta, gain, w_bf16, w_bf16, w_bf16)


RET_C = 256
RET_BATCH = 4


def _retention_kernel(lg_ref, q_ref, k_ref, v_ref, km_ref, vm_ref, o_ref,
                      tq_ref, tk_ref, td_ref, sel_ref, ts_ref, tm_ref, u_ref, s_ref, *, n_chunks):
    pair = pl.program_id(0)
    C = RET_C
    k_scale = RET_DK ** -0.5

    @pl.when(pl.program_id(1) == 0)
    def _build_tables():
        lgf = [lg_ref[0, 2 * pair + hl] for hl in range(2)]
        lgb = [lg_ref[1, 2 * pair + hl] for hl in range(2)]
        row = lax.broadcasted_iota(jnp.int32, (C, LANES), 0).astype(F32)
        first = lax.broadcasted_iota(jnp.int32, (C, LANES), 1) < RET_DK
        pos_t = lax.broadcasted_iota(jnp.int32, (LANES, C), 1).astype(F32)
        first_t = lax.broadcasted_iota(jnp.int32, (LANES, C), 0) < RET_DK
        diff = (lax.broadcasted_iota(jnp.int32, (C, C), 0)
                - lax.broadcasted_iota(jnp.int32, (C, C), 1)).astype(F32)
        srow = lax.broadcasted_iota(jnp.int32, (LANES, 2 * RET_DV), 0)
        scol = lax.broadcasted_iota(jnp.int32, (LANES, 2 * RET_DV), 1)
        mrow = lax.broadcasted_iota(jnp.int32, (N_META, LANES), 0).astype(F32)
        mfirst = lax.broadcasted_iota(jnp.int32, (N_META, LANES), 1) < RET_DK

        def both(sel, fn, lg):
            return jnp.where(sel, fn(lg[0]), fn(lg[1]))

        tq_ref[0] = both(first, lambda l: jnp.exp((row + 1.0) * l), lgf)
        tq_ref[1] = both(first, lambda l: jnp.exp((C - row) * l), lgb)
        tk_ref[0] = both(first_t, lambda l: jnp.exp((C - 1.0 - pos_t) * l), lgf) * k_scale
        tk_ref[1] = both(first_t, lambda l: jnp.exp(pos_t * l), lgb) * k_scale
        tm_ref[...] = both(mfirst, lambda l: jnp.exp((N_META - 1.0 - mrow) * l), lgf) * k_scale
        for hl in range(2):
            td_ref[hl] = k_scale * jnp.where(diff <= 0, jnp.exp(jnp.maximum(-diff, 0.0) * lgf[hl]),
                                             jnp.exp(jnp.maximum(diff, 0.0) * lgb[hl]))
            sel_ref[hl] = jnp.where(first == (hl == 0), 1.0, 0.0).astype(BF16)
        own = (srow < RET_DK) == (scol < RET_DV)
        zeros = jnp.zeros((LANES, 2 * RET_DV), F32)
        ts_ref[0] = jnp.where(own, both(srow < RET_DK, lambda l: jnp.exp(zeros + C * l), lgf), 0.0)
        ts_ref[1] = jnp.where(own, both(srow < RET_DK, lambda l: jnp.exp(zeros + C * l), lgb), 0.0)
        ts_ref[2] = jnp.where(own, 1.0, 0.0)

    def head(x, hl):
        return x[:, hl * RET_DV:(hl + 1) * RET_DV]

    own_mask = ts_ref[2]
    for bb in range(RET_BATCH):
        _retention_one(q_ref.at[bb], k_ref.at[bb], v_ref.at[bb], km_ref, vm_ref, o_ref.at[bb],
                       tq_ref, tk_ref, td_ref, sel_ref, ts_ref, tm_ref, u_ref.at[bb], s_ref.at[bb],
                       own_mask, head, n_chunks)


def _retention_one(q_ref, k_ref, v_ref, km_ref, vm_ref, o_ref, tq_ref, tk_ref, td_ref, sel_ref,
                   ts_ref, tm_ref, u_ref, s_ref, own_mask, head, n_chunks):
    C = RET_C

    for c in range(n_chunks):
        rows = slice(c * C, (c + 1) * C)
        k_t = k_ref[rows, :].astype(F32).T
        lhs = jnp.concatenate([k_t * tk_ref[0], k_t * tk_ref[1]], axis=0).astype(BF16)
        u = _dot(lhs, v_ref[rows, :])
        u_ref[c, 0:LANES, :] = u[0:LANES] * own_mask
        u_ref[c, LANES:2 * LANES, :] = u[LANES:2 * LANES] * own_mask

    sf = _dot_tn((km_ref[...].astype(F32) * tm_ref[...]).astype(BF16), vm_ref[...]) * own_mask
    for c in range(n_chunks):
        s_ref[c, 0:LANES, :] = sf.astype(BF16)
        if c + 1 < n_chunks:
            sf = sf * ts_ref[0] + u_ref[c, 0:LANES, :]
    sb = jnp.zeros((LANES, 2 * RET_DV), F32)
    for c in reversed(range(n_chunks)):
        s_ref[c, LANES:2 * LANES, :] = sb.astype(BF16)
        if c > 0:
            sb = sb * ts_ref[1] + u_ref[c, LANES:2 * LANES, :]

    for c in range(n_chunks):
        rows = slice(c * C, (c + 1) * C)
        q16 = q_ref[rows, :]
        kc = k_ref[rows, :]
        vc = v_ref[rows, :]
        qc = q16.astype(F32)
        q_cross = jnp.concatenate([(qc * tq_ref[0]).astype(BF16),
                                   (qc * tq_ref[1]).astype(BF16)], axis=1)
        cross = _dot(q_cross, s_ref[c])
        for hl in range(2):
            s_t = _dot_nt(kc, q16 * sel_ref[hl]) * td_ref[hl]
            o_t = _dot_tn(head(vc, hl), s_t.astype(BF16))
            o_ref[rows, hl * RET_DV:(hl + 1) * RET_DV] = (head(cross, hl) + o_t.T).astype(BF16)


def _retention_call(log_gammas, proj, proj_meta, batch, seq):
    assert seq % RET_C == 0 and batch % RET_BATCH == 0
    n_chunks = seq // RET_C
    n_pairs = RET_HEADS // 2
    kernel = functools.partial(_retention_kernel, n_chunks=n_chunks)
    qb, kb = OFF_RQ // LANES, OFF_RK // LANES
    vb = OFF_RV // (2 * RET_DV)
    return pl.pallas_call(
        kernel,
        out_shape=jax.ShapeDtypeStruct((batch, seq, RET_W), BF16),
        grid=(n_pairs, batch // RET_BATCH),
        in_specs=[
            pl.BlockSpec(memory_space=pltpu.SMEM),
            pl.BlockSpec((RET_BATCH, seq, LANES), lambda p, b: (b, 0, qb + p)),
            pl.BlockSpec((RET_BATCH, seq, LANES), lambda p, b: (b, 0, kb + p)),
            pl.BlockSpec((RET_BATCH, seq, 2 * RET_DV), lambda p, b: (b, 0, vb + p)),
            pl.BlockSpec((N_META, LANES), lambda p, b: (0, MTOK_RK // LANES + p)),
            pl.BlockSpec((N_META, 2 * RET_DV), lambda p, b: (0, MTOK_RV // (2 * RET_DV) + p)),
        ],
        out_specs=pl.BlockSpec((RET_BATCH, seq, 2 * RET_DV), lambda p, b: (b, 0, p)),
        scratch_shapes=[
            pltpu.VMEM((2, RET_C, LANES), F32),
            pltpu.VMEM((2, LANES, RET_C), F32),
            pltpu.VMEM((2, RET_C, RET_C), F32),
            pltpu.VMEM((2, RET_C, LANES), BF16),
            pltpu.VMEM((3, LANES, 2 * RET_DV), F32),
            pltpu.VMEM((N_META, LANES), F32),
            pltpu.VMEM((RET_BATCH, n_chunks, 2 * LANES, 2 * RET_DV), F32),
            pltpu.VMEM((RET_BATCH, n_chunks, 2 * LANES, 2 * RET_DV), BF16),
        ],
        compiler_params=pltpu.CompilerParams(
            dimension_semantics=("arbitrary", "arbitrary"), vmem_limit_bytes=VMEM_LIMIT),
        name="retention",
    )(log_gammas, proj, proj, proj, proj_meta, proj_meta)


def _pair_head_cols(w):
    w5 = w.reshape(w.shape[0], ATT_KV_HEADS // 2, 2, ATT_GROUP, ATT_DH)
    return w5.transpose(0, 1, 3, 2, 4).reshape(w.shape)


QB = 64
BAND = QB + 2 * WINDOW
MTOK_PAD = 64
N_KEYS = BAND + MTOK_PAD
ATT_ROWS = 2048
N_OFFSETS = 2 * WINDOW // QB + 1
SINK_COL = BAND + N_META
HB_UNROLL = 16


def _attention_kernel(sink_ref, q_ref, k_ref, v_ref, km_ref, vm_ref, *rest, seq, n_cast):
    cast_src, o_ref = rest[:n_cast], rest[n_cast]
    cast_dst, bias_ref = rest[n_cast + 1:2 * n_cast + 1], rest[2 * n_cast + 1]
    for src, dst in zip(cast_src, cast_dst):
        dst[...] = src[...].astype(BF16)
    _attention_body(sink_ref, q_ref, k_ref, v_ref, km_ref, vm_ref, o_ref, bias_ref, seq=seq)


def _attention_body(sink_ref, q_ref, k_ref, v_ref, km_ref, vm_ref, o_ref, bias_ref, *, seq):
    n = pl.program_id(1)

    @pl.when((pl.program_id(0) == 0) & (n == 0))
    def _build_bias():
        r = lax.broadcasted_iota(jnp.int32, (QB, N_KEYS), 0)
        j = lax.broadcasted_iota(jnp.int32, (QB, N_KEYS), 1)
        for t in range(N_OFFSETS):
            dist = jnp.abs(j - t * QB - r)
            band_ok = (j < BAND) & (dist <= WINDOW)
            ndist = dist.astype(F32) * (-LOG2E)
            for head in range(ATT_Q_HEADS):
                slope = 2.0 ** (-8.0 * (head + 1) / ATT_Q_HEADS)
                sink = sink_ref[head] * LOG2E
                rest = jnp.where((j >= BAND) & (j < SINK_COL), 0.0,
                                 jnp.where(j == SINK_COL, sink, NEG_INF))
                bias_ref[t, head] = jnp.where(band_ok, slope * ndist, rest)

    low_k = lax.broadcasted_iota(jnp.int32, (N_KEYS, LANES), 1) < ATT_DH

    def block_diag(x):
        zero = jnp.zeros_like(x)
        return jnp.concatenate([jnp.where(low_k, x, zero), jnp.where(low_k, zero, x)], axis=0)

    ones_bd = jnp.concatenate([jnp.where(low_k, 1.0, 0.0), jnp.where(low_k, 0.0, 1.0)],
                              axis=0).astype(BF16)
    meta_fill = jnp.zeros((MTOK_PAD - N_META, LANES), BF16)

    def body(it, carry):
        for u in range(HB_UNROLL):
            hb = it * HB_UNROLL + u
            row0 = n * ATT_ROWS + hb * QB
            start = pl.multiple_of(jnp.clip(row0 - WINDOW, 0, seq - BAND), QB)
            t = (row0 - start) // QB
            rows = pl.ds(pl.multiple_of(hb * QB, QB), QB)
            for jp in range(ATT_KV_HEADS // 2):
                ksl = slice(jp * LANES, (jp + 1) * LANES)
                k_bd = block_diag(jnp.concatenate(
                    [k_ref[pl.ds(start, BAND), ksl], km_ref[:, ksl], meta_fill], axis=0))
                v_bd = jnp.concatenate([block_diag(jnp.concatenate(
                    [v_ref[pl.ds(start, BAND), ksl], vm_ref[:, ksl], meta_fill], axis=0)),
                    ones_bd], axis=1)
                groups = [slice((jp * ATT_GROUP + g) * LANES, (jp * ATT_GROUP + g + 1) * LANES)
                          for g in range(ATT_GROUP)]
                q_stack = jnp.concatenate([q_ref[rows, grp] for grp in groups], axis=0)
                s = _dot_nt(q_stack, k_bd)
                probs = []
                for g in range(ATT_GROUP):
                    p_halves = []
                    for half in range(2):
                        head = (2 * jp + half) * ATT_GROUP + g
                        sg = (s[g * QB:(g + 1) * QB, half * N_KEYS:(half + 1) * N_KEYS]
                              + bias_ref[t, head])
                        p = jnp.exp2(sg - jnp.max(sg, axis=-1, keepdims=True))
                        p_halves.append(p.astype(BF16))
                    probs.append(jnp.concatenate(p_halves, axis=1))
                o = _dot(jnp.concatenate(probs, axis=0), v_bd)
                out = o[:, 0:LANES] / o[:, LANES:2 * LANES]
                for g in range(ATT_GROUP):
                    o_ref[rows, groups[g]] = out[g * QB:(g + 1) * QB].astype(BF16)
        return carry

    lax.fori_loop(0, ATT_ROWS // QB // HB_UNROLL, body, 0)


def _paired_src_block(i, blocks_per_group):
    grp, sub = i // blocks_per_group, i % blocks_per_group
    j, g, half = grp // (2 * ATT_GROUP), (grp // 2) % ATT_GROUP, grp % 2
    return ((j * 2 + half) * ATT_GROUP + g) * blocks_per_group + sub


def _attention_call(sink, proj, proj_meta, w_casts, batch, seq):
    assert seq % ATT_ROWS == 0 and seq >= BAND
    n_blk = seq // ATT_ROWS
    steps = batch * n_blk
    kernel = functools.partial(_attention_kernel, seq=seq, n_cast=len(w_casts))
    kv_w = ATT_KV_HEADS * ATT_DH
    step = lambda b, n: b * n_blk + n
    cast_in, cast_out, cast_shapes = [], [], []
    for idx, w in enumerate(w_casts):
        rows, cols = w.shape
        reps = 1
        while rows % (steps // reps) or (rows // (steps // reps)) % 16:
            reps *= 2
        blk = rows // (steps // reps)
        dst_map = lambda b, n, reps=reps: (step(b, n) // reps, 0)
        if idx == 1:
            assert reps == 1 and ATT_DH % blk == 0
            src_map = lambda b, n, bpg=ATT_DH // blk: (_paired_src_block(step(b, n), bpg), 0)
        else:
            src_map = dst_map
        cast_in.append(pl.BlockSpec((blk, cols), src_map))
        cast_out.append(pl.BlockSpec((blk, cols), dst_map))
        cast_shapes.append(jax.ShapeDtypeStruct((rows, cols), BF16))
    return pl.pallas_call(
        kernel,
        out_shape=[jax.ShapeDtypeStruct((batch, seq, ATT_W), BF16)] + cast_shapes,
        grid=(batch, n_blk),
        in_specs=[
            pl.BlockSpec(memory_space=pltpu.SMEM),
            pl.BlockSpec((None, ATT_ROWS, ATT_W), lambda b, n: (b, n, OFF_AQ // ATT_W)),
            pl.BlockSpec((None, seq, kv_w), lambda b, n: (b, 0, OFF_AK // kv_w)),
            pl.BlockSpec((None, seq, kv_w), lambda b, n: (b, 0, OFF_AV // kv_w)),
            pl.BlockSpec((N_META, kv_w), lambda b, n: (0, MTOK_AK // kv_w)),
            pl.BlockSpec((N_META, kv_w), lambda b, n: (0, MTOK_AV // kv_w)),
        ] + cast_in,
        out_specs=[pl.BlockSpec((None, ATT_ROWS, ATT_W), lambda b, n: (b, n, 0))] + cast_out,
        scratch_shapes=[pltpu.VMEM((N_OFFSETS, ATT_Q_HEADS, QB, N_KEYS), F32)],
        compiler_params=pltpu.CompilerParams(
            dimension_semantics=("arbitrary", "arbitrary"), vmem_limit_bytes=VMEM_LIMIT),
        name="attention",
    )(sink, proj, proj, proj, proj_meta, proj_meta, *w_casts)


FF_CHUNK = 256
GATE_HALF = D_MODEL // 2


def _mixffn_kernel(x_ref, ret_ref, rg_ref, gain_ref, att_ref, ga0_ref, ga1_ref, gb0_ref, gb1_ref,
                   wr_ref, wa_ref, wo_ref, nf_ref, wg_ref, wu_ref, wd_ref, nfin_ref, o_ref,
                   r_ref, h_ref, act_ref):
    gb = jnp.concatenate([gb0_ref[...], gb1_ref[...]], axis=1).astype(F32)
    att_part = _sigmoid(gb) * _dot(att_ref[...], wa_ref[...])
    ret_part = None
    for h in range(RET_HEADS):
        sl = slice(h * RET_DV, (h + 1) * RET_DV)
        o = ret_ref[:, sl].astype(F32)
        d = o - jnp.mean(o, axis=-1, keepdims=True)
        var = jnp.mean(d * d, axis=-1, keepdims=True)
        g = rg_ref[:, sl].astype(F32)
        r_ref[:, sl] = (g * _sigmoid(g) * (d * lax.rsqrt(var + GN_EPS) * gain_ref[:, sl])).astype(BF16)
        if h % 2 == 1:
            ks = slice((h - 1) * RET_DV, (h + 1) * RET_DV)
            part = _dot(r_ref[:, ks], wr_ref[ks, :])
            ret_part = part if ret_part is None else ret_part + part
    ga = jnp.concatenate([ga0_ref[...], ga1_ref[...]], axis=1).astype(F32)
    merged = _sigmoid(ga) * ret_part + att_part
    x = x_ref[...] + _dot(merged.astype(BF16), wo_ref[...])
    ms = jnp.mean(x * x, axis=-1, keepdims=True)
    h_ref[...] = (x * lax.rsqrt(ms + RMS_EPS) * nf_ref[...]).astype(BF16)
    for c in range(0, D_FF, FF_CHUNK):
        a = _dot(h_ref[...], wg_ref[:, c:c + FF_CHUNK])
        b = _dot(h_ref[...], wu_ref[:, c:c + FF_CHUNK])
        act_ref[:, c:c + FF_CHUNK] = (a * _sigmoid(a) * b).astype(BF16)
    y = x + _dot(act_ref[...], wd_ref[...])
    ms = jnp.mean(y * y, axis=-1, keepdims=True)
    o_ref[...] = y * lax.rsqrt(ms + RMS_EPS) * nfin_ref[...]


def _mixffn_call(x2d, ret, gn_gain, att, proj, wr, wa, wo, norm_ffn, wgu, wd, norm_final, tm):
    rows = x2d.shape[0]
    const = lambda shape, col=0: pl.BlockSpec(shape, lambda i: (0, col),
                                              pipeline_mode=pl.Buffered(1))
    row_tile = lambda col: pl.BlockSpec((tm, D_MODEL), lambda i: (i, col))
    half_tile = lambda col: pl.BlockSpec((tm, GATE_HALF), lambda i: (i, col))
    return pl.pallas_call(
        _mixffn_kernel,
        out_shape=jax.ShapeDtypeStruct((rows, D_MODEL), F32),
        grid=(rows // tm,),
        in_specs=[
            row_tile(0), row_tile(0), row_tile(OFF_RG // RET_W), const((1, RET_W)), row_tile(0),
            half_tile(OFF_GA // GATE_HALF), half_tile(OFF_GA // GATE_HALF + 1),
            half_tile(OFF_GB // GATE_HALF), half_tile(OFF_GB // GATE_HALF + 1),
            const((RET_W, D_MODEL)), const((ATT_W, D_MODEL)), const((D_MODEL, D_MODEL)),
            const((1, D_MODEL)),
            const((D_MODEL, D_FF), 0), const((D_MODEL, D_FF), 1), const((D_FF, D_MODEL)),
            const((1, D_MODEL)),
        ],
        out_specs=row_tile(0),
        scratch_shapes=[pltpu.VMEM((tm, RET_W), BF16), pltpu.VMEM((tm, D_MODEL), BF16),
                        pltpu.VMEM((tm, D_FF), BF16)],
        compiler_params=pltpu.CompilerParams(
            dimension_semantics=("parallel",), vmem_limit_bytes=VMEM_LIMIT),
        name="mixffn",
    )(x2d, ret, proj, gn_gain, att, proj, proj, proj, proj, wr, wa, wo, norm_ffn, wgu, wgu, wd,
      norm_final)


def kernel(x, meta_tokens, w_in, ret_decay_logit_fwd, ret_decay_logit_bwd, ret_gn_gain, attn_sink,
           w_branch_ret, w_branch_att, w_out, norm_mix, norm_ffn, w_gate_up, w_down, norm_final):
    batch, seq, d = x.shape
    assert d == D_MODEL and w_in.shape[0] == 1

    w_in0 = w_in[0]
    w_in_b = w_in0.astype(BF16)
    w_q_b = _pair_head_cols(w_in0[:, OFF_AQ:OFF_AK]).astype(BF16)
    nmix = norm_mix[0].reshape(1, D_MODEL)
    nffn = norm_ffn[0].reshape(1, D_MODEL)
    nfin = norm_final.reshape(1, D_MODEL)

    x2d = x.reshape(batch * seq, D_MODEL)
    proj = _inproj_call(x2d, nmix, w_in_b, w_q_b, INPROJ_ROWS)
    proj_meta = _meta_call(meta_tokens.astype(x.dtype), nmix, w_in_b)
    proj3 = proj.reshape(batch, seq, IN_COLS)

    log_gammas = jnp.stack([jax.nn.log_sigmoid(ret_decay_logit_fwd[0].astype(F32)),
                            jax.nn.log_sigmoid(ret_decay_logit_bwd[0].astype(F32))])
    ret = _retention_call(log_gammas, proj3, proj_meta, batch, seq)

    att, wr, wa, wo, wgu, wd = _attention_call(
        attn_sink[0].astype(F32), proj3, proj_meta,
        (w_branch_ret[0], w_branch_att[0], w_out[0], w_gate_up[0], w_down[0]), batch, seq)

    out = _mixffn_call(x2d, ret.reshape(batch * seq, RET_W), ret_gn_gain[0].reshape(1, RET_W),
                       att.reshape(batch * seq, ATT_W), proj,
                       wr, wa, wo, nffn, wgu, wd, nfin, MIXFFN_ROWS)
    return out.reshape(batch, seq, D_MODEL)
```

```python
import functools

import numpy as np
import jax
import jax.numpy as jnp
from jax import lax
from jax.experimental import pallas as pl
from jax.experimental.pallas import tpu as pltpu

D_MODEL = 1024
N_META = 16
RET_HEADS = 8
RET_DK = 64
RET_DV = 128
RET_W = RET_HEADS * RET_DV
ATT_Q_HEADS = 16
ATT_KV_HEADS = 4
ATT_GROUP = ATT_Q_HEADS // ATT_KV_HEADS
ATT_DH = 64
ATT_W = ATT_Q_HEADS * ATT_DH
WINDOW = 128
D_FF = 2816
IN_SIZES = (512, 512, 1024, 1024, 1024, 256, 256, 1024, 1024)
IN_COLS = sum(IN_SIZES)
OFF_RQ, OFF_RK, OFF_RV, OFF_RG, OFF_AQ, OFF_AK, OFF_AV, OFF_GA, OFF_GB = (
    int(c) for c in np.cumsum((0,) + IN_SIZES[:-1]))
RMS_EPS = 1e-6
GN_EPS = 1e-5
NEG_INF = -1e30
LOG2E = 1.4426950408889634
ATT_Q_SCALE = ATT_DH ** -0.5 * LOG2E

LANES = 128
VMEM_LIMIT = 56 * 1024 * 1024
INPROJ_ROWS = 1024
MERGE_ROWS = 512
FFN_ROWS = 1024

F32 = jnp.float32
BF16 = jnp.bfloat16


def _sigmoid(x):
    return 1.0 / (1.0 + jnp.exp(-x))


def _dot(a, b):
    return jnp.dot(a, b, preferred_element_type=F32)


def _dot_nt(a, b):
    return lax.dot_general(a, b, (((1,), (1,)), ((), ())), preferred_element_type=F32)


def _dot_tn(a, b):
    return lax.dot_general(a, b, (((0,), (0,)), ((), ())), preferred_element_type=F32)


IN_CHUNK = 512
IN_SUB = 256


def _inproj_kernel(x_ref, g_ref, w_ref, wq_ref, o_ref, h_ref):
    tm = x_ref.shape[0]
    sub = min(IN_SUB, tm)
    for r0 in range(0, tm, sub):
        rows = slice(r0, r0 + sub)
        x = x_ref[rows, :]
        ms = jnp.mean(x * x, axis=-1, keepdims=True)
        h_ref[rows, :] = (x * lax.rsqrt(ms + RMS_EPS) * g_ref[...]).astype(BF16)
        for c in range(0, IN_COLS, IN_CHUNK):
            if OFF_AQ <= c < OFF_AQ + ATT_W:
                q0 = c - OFF_AQ
                acc = _dot(h_ref[rows, :], wq_ref[:, q0:q0 + IN_CHUNK]) * ATT_Q_SCALE
            else:
                acc = _dot(h_ref[rows, :], w_ref[:, c:c + IN_CHUNK])
            o_ref[rows, c:c + IN_CHUNK] = acc.astype(BF16)


def _inproj_call(x2d, gain, w_bf16, wq_bf16, tm):
    rows = x2d.shape[0]
    const = lambda shape: pl.BlockSpec(shape, lambda i: (0, 0), pipeline_mode=pl.Buffered(1))
    return pl.pallas_call(
        _inproj_kernel,
        out_shape=jax.ShapeDtypeStruct((rows, IN_COLS), BF16),
        grid=(rows // tm,),
        in_specs=[
            pl.BlockSpec((tm, D_MODEL), lambda i: (i, 0)),
            pl.BlockSpec((1, D_MODEL), lambda i: (0, 0)),
            const((D_MODEL, IN_COLS)), const((D_MODEL, ATT_W)),
        ],
        out_specs=pl.BlockSpec((tm, IN_COLS), lambda i: (i, 0)),
        scratch_shapes=[pltpu.VMEM((tm, D_MODEL), BF16)],
        compiler_params=pltpu.CompilerParams(
            dimension_semantics=("parallel",), vmem_limit_bytes=VMEM_LIMIT),
        name="inproj",
    )(x2d, gain, w_bf16, wq_bf16)


MTOK_RK = 0
MTOK_RV = MTOK_RK + RET_HEADS * RET_DK
MTOK_AK = MTOK_RV + RET_W
MTOK_AV = MTOK_AK + ATT_KV_HEADS * ATT_DH
MTOK_COLS = MTOK_AV + ATT_KV_HEADS * ATT_DH


def _meta_kernel(x_ref, g_ref, wk_ref, wv_ref, wa_ref, o_ref):
    x = x_ref[...]
    ms = jnp.mean(x * x, axis=-1, keepdims=True)
    h = (x * lax.rsqrt(ms + RMS_EPS) * g_ref[...]).astype(BF16)
    o_ref[:, MTOK_RK:MTOK_RV] = _dot(h, wk_ref[...]).astype(BF16)
    o_ref[:, MTOK_RV:MTOK_AK] = _dot(h, wv_ref[...]).astype(BF16)
    o_ref[:, MTOK_AK:MTOK_COLS] = _dot(h, wa_ref[...]).astype(BF16)


def _meta_call(meta, gain, w_bf16):
    kv_w = ATT_KV_HEADS * ATT_DH
    assert OFF_AV == OFF_AK + kv_w
    piece = lambda width, off: pl.BlockSpec((D_MODEL, width), lambda i: (0, off // width))
    return pl.pallas_call(
        _meta_kernel,
        out_shape=jax.ShapeDtypeStruct((N_META, MTOK_COLS), BF16),
        grid=(1,),
        in_specs=[
            pl.BlockSpec((N_META, D_MODEL), lambda i: (0, 0)),
            pl.BlockSpec((1, D_MODEL), lambda i: (0, 0)),
            piece(RET_HEADS * RET_DK, OFF_RK), piece(RET_W, OFF_RV), piece(2 * kv_w, OFF_AK),
        ],
        out_specs=pl.BlockSpec((N_META, MTOK_COLS), lambda i: (0, 0)),
        name="metaproj",
    )(meta, gain, w_bf16, w_bf16, w_bf16)


RET_C = 256
RET_BATCH = 4


def _retention_kernel(lg_ref, q_ref, k_ref, v_ref, km_ref, vm_ref, o_ref,
                      tq_ref, tk_ref, td_ref, sel_ref, ts_ref, tm_ref, u_ref, s_ref, *, n_chunks):
    pair = pl.program_id(0)
    C = RET_C
    k_scale = RET_DK ** -0.5

    @pl.when(pl.program_id(1) == 0)
    def _build_tables():
        lgf = [lg_ref[0, 2 * pair + hl] for hl in range(2)]
        lgb = [lg_ref[1, 2 * pair + hl] for hl in range(2)]
        row = lax.broadcasted_iota(jnp.int32, (C, LANES), 0).astype(F32)
        first = lax.broadcasted_iota(jnp.int32, (C, LANES), 1) < RET_DK
        pos_t = lax.broadcasted_iota(jnp.int32, (LANES, C), 1).astype(F32)
        first_t = lax.broadcasted_iota(jnp.int32, (LANES, C), 0) < RET_DK
        diff = (lax.broadcasted_iota(jnp.int32, (C, C), 0)
                - lax.broadcasted_iota(jnp.int32, (C, C), 1)).astype(F32)
        srow = lax.broadcasted_iota(jnp.int32, (LANES, 2 * RET_DV), 0)
        scol = lax.broadcasted_iota(jnp.int32, (LANES, 2 * RET_DV), 1)
        mrow = lax.broadcasted_iota(jnp.int32, (N_META, LANES), 0).astype(F32)
        mfirst = lax.broadcasted_iota(jnp.int32, (N_META, LANES), 1) < RET_DK

        def both(sel, fn, lg):
            return jnp.where(sel, fn(lg[0]), fn(lg[1]))

        tq_ref[0] = both(first, lambda l: jnp.exp((row + 1.0) * l), lgf)
        tq_ref[1] = both(first, lambda l: jnp.exp((C - row) * l), lgb)
        tk_ref[0] = both(first_t, lambda l: jnp.exp((C - 1.0 - pos_t) * l), lgf) * k_scale
        tk_ref[1] = both(first_t, lambda l: jnp.exp(pos_t * l), lgb) * k_scale
        tm_ref[...] = both(mfirst, lambda l: jnp.exp((N_META - 1.0 - mrow) * l), lgf) * k_scale
        for hl in range(2):
            td_ref[hl] = k_scale * jnp.where(diff <= 0, jnp.exp(jnp.maximum(-diff, 0.0) * lgf[hl]),
                                             jnp.exp(jnp.maximum(diff, 0.0) * lgb[hl]))
            sel_ref[hl] = jnp.where(first == (hl == 0), 1.0, 0.0).astype(BF16)
        own = (srow < RET_DK) == (scol < RET_DV)
        zeros = jnp.zeros((LANES, 2 * RET_DV), F32)
        ts_ref[0] = jnp.where(own, both(srow < RET_DK, lambda l: jnp.exp(zeros + C * l), lgf), 0.0)
        ts_ref[1] = jnp.where(own, both(srow < RET_DK, lambda l: jnp.exp(zeros + C * l), lgb), 0.0)
        ts_ref[2] = jnp.where(own, 1.0, 0.0)

    def head(x, hl):
        return x[:, hl * RET_DV:(hl + 1) * RET_DV]

    own_mask = ts_ref[2]
    for bb in range(RET_BATCH):
        _retention_one(q_ref.at[bb], k_ref.at[bb], v_ref.at[bb], km_ref, vm_ref, o_ref.at[bb],
                       tq_ref, tk_ref, td_ref, sel_ref, ts_ref, tm_ref, u_ref.at[bb], s_ref.at[bb],
                       own_mask, head, n_chunks)


def _retention_one(q_ref, k_ref, v_ref, km_ref, vm_ref, o_ref, tq_ref, tk_ref, td_ref, sel_ref,
                   ts_ref, tm_ref, u_ref, s_ref, own_mask, head, n_chunks):
    C = RET_C

    for c in range(n_chunks):
        rows = slice(c * C, (c + 1) * C)
        k_t = k_ref[rows, :].astype(F32).T
        lhs = jnp.concatenate([k_t * tk_ref[0], k_t * tk_ref[1]], axis=0).astype(BF16)
        u = _dot(lhs, v_ref[rows, :])
        u_ref[c, 0:LANES, :] = u[0:LANES] * own_mask
        u_ref[c, LANES:2 * LANES, :] = u[LANES:2 * LANES] * own_mask

    sf = _dot_tn((km_ref[...].astype(F32) * tm_ref[...]).astype(BF16), vm_ref[...]) * own_mask
    for c in range(n_chunks):
        s_ref[c, 0:LANES, :] = sf.astype(BF16)
        if c + 1 < n_chunks:
            sf = sf * ts_ref[0] + u_ref[c, 0:LANES, :]
    sb = jnp.zeros((LANES, 2 * RET_DV), F32)
    for c in reversed(range(n_chunks)):
        s_ref[c, LANES:2 * LANES, :] = sb.astype(BF16)
        if c > 0:
            sb = sb * ts_ref[1] + u_ref[c, LANES:2 * LANES, :]

    for c in range(n_chunks):
        rows = slice(c * C, (c + 1) * C)
        q16 = q_ref[rows, :]
        kc = k_ref[rows, :]
        vc = v_ref[rows, :]
        qc = q16.astype(F32)
        q_cross = jnp.concatenate([(qc * tq_ref[0]).astype(BF16),
                                   (qc * tq_ref[1]).astype(BF16)], axis=1)
        cross = _dot(q_cross, s_ref[c])
        for hl in range(2):
            s_t = _dot_nt(kc, q16 * sel_ref[hl]) * td_ref[hl]
            o_t = _dot_tn(head(vc, hl), s_t.astype(BF16))
            o_ref[rows, hl * RET_DV:(hl + 1) * RET_DV] = (head(cross, hl) + o_t.T).astype(BF16)


def _retention_call(log_gammas, proj, proj_meta, batch, seq):
    assert seq % RET_C == 0 and batch % RET_BATCH == 0
    n_chunks = seq // RET_C
    n_pairs = RET_HEADS // 2
    kernel = functools.partial(_retention_kernel, n_chunks=n_chunks)
    qb, kb = OFF_RQ // LANES, OFF_RK // LANES
    vb = OFF_RV // (2 * RET_DV)
    return pl.pallas_call(
        kernel,
        out_shape=jax.ShapeDtypeStruct((batch, seq, RET_W), BF16),
        grid=(n_pairs, batch // RET_BATCH),
        in_specs=[
            pl.BlockSpec(memory_space=pltpu.SMEM),
            pl.BlockSpec((RET_BATCH, seq, LANES), lambda p, b: (b, 0, qb + p)),
            pl.BlockSpec((RET_BATCH, seq, LANES), lambda p, b: (b, 0, kb + p)),
            pl.BlockSpec((RET_BATCH, seq, 2 * RET_DV), lambda p, b: (b, 0, vb + p)),
            pl.BlockSpec((N_META, LANES), lambda p, b: (0, MTOK_RK // LANES + p)),
            pl.BlockSpec((N_META, 2 * RET_DV), lambda p, b: (0, MTOK_RV // (2 * RET_DV) + p)),
        ],
        out_specs=pl.BlockSpec((RET_BATCH, seq, 2 * RET_DV), lambda p, b: (b, 0, p)),
        scratch_shapes=[
            pltpu.VMEM((2, RET_C, LANES), F32),
            pltpu.VMEM((2, LANES, RET_C), F32),
            pltpu.VMEM((2, RET_C, RET_C), F32),
            pltpu.VMEM((2, RET_C, LANES), BF16),
            pltpu.VMEM((3, LANES, 2 * RET_DV), F32),
            pltpu.VMEM((N_META, LANES), F32),
            pltpu.VMEM((RET_BATCH, n_chunks, 2 * LANES, 2 * RET_DV), F32),
            pltpu.VMEM((RET_BATCH, n_chunks, 2 * LANES, 2 * RET_DV), BF16),
        ],
        compiler_params=pltpu.CompilerParams(
            dimension_semantics=("arbitrary", "arbitrary"), vmem_limit_bytes=VMEM_LIMIT),
        name="retention",
    )(log_gammas, proj, proj, proj, proj_meta, proj_meta)


def _pair_head_cols(w):
    w5 = w.reshape(w.shape[0], ATT_KV_HEADS // 2, 2, ATT_GROUP, ATT_DH)
    return w5.transpose(0, 1, 3, 2, 4).reshape(w.shape)


QB = 64
BAND = QB + 2 * WINDOW
MTOK_PAD = 64
N_KEYS = BAND + MTOK_PAD
ATT_ROWS = 2048
N_OFFSETS = 2 * WINDOW // QB + 1
SINK_COL = BAND + N_META
HB_UNROLL = 16


def _attention_kernel(sink_ref, q_ref, k_ref, v_ref, km_ref, vm_ref, *rest, seq, n_cast):
    cast_src, o_ref = rest[:n_cast], rest[n_cast]
    cast_dst, bias_ref = rest[n_cast + 1:2 * n_cast + 1], rest[2 * n_cast + 1]
    for src, dst in zip(cast_src, cast_dst):
        dst[...] = src[...].astype(BF16)
    _attention_body(sink_ref, q_ref, k_ref, v_ref, km_ref, vm_ref, o_ref, bias_ref, seq=seq)


def _attention_body(sink_ref, q_ref, k_ref, v_ref, km_ref, vm_ref, o_ref, bias_ref, *, seq):
    n = pl.program_id(1)

    @pl.when((pl.program_id(0) == 0) & (n == 0))
    def _build_bias():
        r = lax.broadcasted_iota(jnp.int32, (QB, N_KEYS), 0)
        j = lax.broadcasted_iota(jnp.int32, (QB, N_KEYS), 1)
        for t in range(N_OFFSETS):
            dist = jnp.abs(j - t * QB - r)
            band_ok = (j < BAND) & (dist <= WINDOW)
            ndist = dist.astype(F32) * (-LOG2E)
            for head in range(ATT_Q_HEADS):
                slope = 2.0 ** (-8.0 * (head + 1) / ATT_Q_HEADS)
                sink = sink_ref[head] * LOG2E
                rest = jnp.where((j >= BAND) & (j < SINK_COL), 0.0,
                                 jnp.where(j == SINK_COL, sink, NEG_INF))
                bias_ref[t, head] = jnp.where(band_ok, slope * ndist, rest)

    low_k = lax.broadcasted_iota(jnp.int32, (N_KEYS, LANES), 1) < ATT_DH

    def block_diag(x):
        zero = jnp.zeros_like(x)
        return jnp.concatenate([jnp.where(low_k, x, zero), jnp.where(low_k, zero, x)], axis=0)

    ones_bd = jnp.concatenate([jnp.where(low_k, 1.0, 0.0), jnp.where(low_k, 0.0, 1.0)],
                              axis=0).astype(BF16)
    meta_fill = jnp.zeros((MTOK_PAD - N_META, LANES), BF16)

    def body(it, carry):
        for u in range(HB_UNROLL):
            hb = it * HB_UNROLL + u
            row0 = n * ATT_ROWS + hb * QB
            start = pl.multiple_of(jnp.clip(row0 - WINDOW, 0, seq - BAND), QB)
            t = (row0 - start) // QB
            rows = pl.ds(pl.multiple_of(hb * QB, QB), QB)
            for jp in range(ATT_KV_HEADS // 2):
                ksl = slice(jp * LANES, (jp + 1) * LANES)
                k_bd = block_diag(jnp.concatenate(
                    [k_ref[pl.ds(start, BAND), ksl], km_ref[:, ksl], meta_fill], axis=0))
                v_bd = jnp.concatenate([block_diag(jnp.concatenate(
                    [v_ref[pl.ds(start, BAND), ksl], vm_ref[:, ksl], meta_fill], axis=0)),
                    ones_bd], axis=1)
                groups = [slice((jp * ATT_GROUP + g) * LANES, (jp * ATT_GROUP + g + 1) * LANES)
                          for g in range(ATT_GROUP)]
                q_stack = jnp.concatenate([q_ref[rows, grp] for grp in groups], axis=0)
                s = _dot_nt(q_stack, k_bd)
                probs = []
                for g in range(ATT_GROUP):
                    p_halves = []
                    for half in range(2):
                        head = (2 * jp + half) * ATT_GROUP + g
                        sg = (s[g * QB:(g + 1) * QB, half * N_KEYS:(half + 1) * N_KEYS]
                              + bias_ref[t, head])
                        p = jnp.exp2(sg - jnp.max(sg, axis=-1, keepdims=True))
                        p_halves.append(p.astype(BF16))
                    probs.append(jnp.concatenate(p_halves, axis=1))
                o = _dot(jnp.concatenate(probs, axis=0), v_bd)
                out = o[:, 0:LANES] / o[:, LANES:2 * LANES]
                for g in range(ATT_GROUP):
                    o_ref[rows, groups[g]] = out[g * QB:(g + 1) * QB].astype(BF16)
        return carry

    lax.fori_loop(0, ATT_ROWS // QB // HB_UNROLL, body, 0)


def _paired_src_block(i, blocks_per_group):
    grp, sub = i // blocks_per_group, i % blocks_per_group
    j, g, half = grp // (2 * ATT_GROUP), (grp // 2) % ATT_GROUP, grp % 2
    return ((j * 2 + half) * ATT_GROUP + g) * blocks_per_group + sub


def _attention_call(sink, proj, proj_meta, w_casts, batch, seq):
    assert seq % ATT_ROWS == 0 and seq >= BAND
    n_blk = seq // ATT_ROWS
    steps = batch * n_blk
    kernel = functools.partial(_attention_kernel, seq=seq, n_cast=len(w_casts))
    kv_w = ATT_KV_HEADS * ATT_DH
    step = lambda b, n: b * n_blk + n
    cast_in, cast_out, cast_shapes = [], [], []
    for idx, w in enumerate(w_casts):
        rows, cols = w.shape
        reps = 1
        while rows % (steps // reps) or (rows // (steps // reps)) % 16:
            reps *= 2
        blk = rows // (steps // reps)
        dst_map = lambda b, n, reps=reps: (step(b, n) // reps, 0)
        if idx == 1:
            assert reps == 1 and ATT_DH % blk == 0
            src_map = lambda b, n, bpg=ATT_DH // blk: (_paired_src_block(step(b, n), bpg), 0)
        else:
            src_map = dst_map
        cast_in.append(pl.BlockSpec((blk, cols), src_map))
        cast_out.append(pl.BlockSpec((blk, cols), dst_map))
        cast_shapes.append(jax.ShapeDtypeStruct((rows, cols), BF16))
    return pl.pallas_call(
        kernel,
        out_shape=[jax.ShapeDtypeStruct((batch, seq, ATT_W), BF16)] + cast_shapes,
        grid=(batch, n_blk),
        in_specs=[
            pl.BlockSpec(memory_space=pltpu.SMEM),
            pl.BlockSpec((None, ATT_ROWS, ATT_W), lambda b, n: (b, n, OFF_AQ // ATT_W)),
            pl.BlockSpec((None, seq, kv_w), lambda b, n: (b, 0, OFF_AK // kv_w)),
            pl.BlockSpec((None, seq, kv_w), lambda b, n: (b, 0, OFF_AV // kv_w)),
            pl.BlockSpec((N_META, kv_w), lambda b, n: (0, MTOK_AK // kv_w)),
            pl.BlockSpec((N_META, kv_w), lambda b, n: (0, MTOK_AV // kv_w)),
        ] + cast_in,
        out_specs=[pl.BlockSpec((None, ATT_ROWS, ATT_W), lambda b, n: (b, n, 0))] + cast_out,
        scratch_shapes=[pltpu.VMEM((N_OFFSETS, ATT_Q_HEADS, QB, N_KEYS), F32)],
        compiler_params=pltpu.CompilerParams(
            dimension_semantics=("arbitrary", "arbitrary"), vmem_limit_bytes=VMEM_LIMIT),
        name="attention",
    )(sink, proj, proj, proj, proj_meta, proj_meta, *w_casts)


FF_CHUNK = 256
GATE_HALF = D_MODEL // 2


def _merge_kernel(x_ref, ret_ref, rg_ref, gain_ref, att_ref, ga0_ref, ga1_ref, gb0_ref, gb1_ref,
                  wr_ref, wa_ref, wo_ref, nf_ref, x2_ref, h_ref, r_ref):
    gb = jnp.concatenate([gb0_ref[...], gb1_ref[...]], axis=1).astype(F32)
    att_part = _sigmoid(gb) * _dot(att_ref[...], wa_ref[...])
    ret_part = None
    for h in range(RET_HEADS):
        sl = slice(h * RET_DV, (h + 1) * RET_DV)
        o = ret_ref[:, sl].astype(F32)
        d = o - jnp.mean(o, axis=-1, keepdims=True)
        var = jnp.mean(d * d, axis=-1, keepdims=True)
        g = rg_ref[:, sl].astype(F32)
        r_ref[:, sl] = (g * _sigmoid(g) * (d * lax.rsqrt(var + GN_EPS) * gain_ref[:, sl])).astype(BF16)
        if h % 2 == 1:
            ks = slice((h - 1) * RET_DV, (h + 1) * RET_DV)
            part = _dot(r_ref[:, ks], wr_ref[ks, :])
            ret_part = part if ret_part is None else ret_part + part
    ga = jnp.concatenate([ga0_ref[...], ga1_ref[...]], axis=1).astype(F32)
    merged = _sigmoid(ga) * ret_part + att_part
    x = x_ref[...] + _dot(merged.astype(BF16), wo_ref[...])
    x2_ref[...] = x
    ms = jnp.mean(x * x, axis=-1, keepdims=True)
    h_ref[...] = (x * lax.rsqrt(ms + RMS_EPS) * nf_ref[...]).astype(BF16)


def _ffn_kernel(x_ref, h_ref, wg_ref, wu_ref, wd_ref, nfin_ref, o_ref, act_ref):
    for c in range(0, D_FF, FF_CHUNK):
        a = _dot(h_ref[...], wg_ref[:, c:c + FF_CHUNK])
        b = _dot(h_ref[...], wu_ref[:, c:c + FF_CHUNK])
        act_ref[:, c:c + FF_CHUNK] = (a * _sigmoid(a) * b).astype(BF16)
    y = x_ref[...] + _dot(act_ref[...], wd_ref[...])
    ms = jnp.mean(y * y, axis=-1, keepdims=True)
    o_ref[...] = y * lax.rsqrt(ms + RMS_EPS) * nfin_ref[...]


def _mixffn_call(x2d, ret, gn_gain, att, proj, wr, wa, wo, norm_ffn, wgu, wd, norm_final):
    rows = x2d.shape[0]
    const = lambda shape, col=0: pl.BlockSpec(shape, lambda i: (0, col),
                                              pipeline_mode=pl.Buffered(1))
    params = pltpu.CompilerParams(dimension_semantics=("parallel",), vmem_limit_bytes=VMEM_LIMIT)
    tm = MERGE_ROWS
    row_tile = lambda col: pl.BlockSpec((MERGE_ROWS, D_MODEL), lambda i: (i, col))
    half_tile = lambda col: pl.BlockSpec((MERGE_ROWS, GATE_HALF), lambda i: (i, col))
    x2, h = pl.pallas_call(
        _merge_kernel,
        out_shape=[jax.ShapeDtypeStruct((rows, D_MODEL), F32),
                   jax.ShapeDtypeStruct((rows, D_MODEL), BF16)],
        grid=(rows // tm,),
        in_specs=[
            row_tile(0), row_tile(0), row_tile(OFF_RG // RET_W), const((1, RET_W)), row_tile(0),
            half_tile(OFF_GA // GATE_HALF), half_tile(OFF_GA // GATE_HALF + 1),
            half_tile(OFF_GB // GATE_HALF), half_tile(OFF_GB // GATE_HALF + 1),
            const((RET_W, D_MODEL)), const((ATT_W, D_MODEL)), const((D_MODEL, D_MODEL)),
            const((1, D_MODEL)),
        ],
        out_specs=[row_tile(0), row_tile(0)],
        scratch_shapes=[pltpu.VMEM((tm, RET_W), BF16)],
        compiler_params=params,
        name="merge",
    )(x2d, ret, proj, gn_gain, att, proj, proj, proj, proj, wr, wa, wo, norm_ffn)
    ffn_tile = pl.BlockSpec((FFN_ROWS, D_MODEL), lambda i: (i, 0))
    return pl.pallas_call(
        _ffn_kernel,
        out_shape=jax.ShapeDtypeStruct((rows, D_MODEL), F32),
        grid=(rows // FFN_ROWS,),
        in_specs=[
            ffn_tile, ffn_tile,
            const((D_MODEL, D_FF), 0), const((D_MODEL, D_FF), 1), const((D_FF, D_MODEL)),
            const((1, D_MODEL)),
        ],
        out_specs=ffn_tile,
        scratch_shapes=[pltpu.VMEM((FFN_ROWS, D_FF), BF16)],
        compiler_params=params,
        name="ffn",
    )(x2, h, wgu, wgu, wd, norm_final)


def kernel(x, meta_tokens, w_in, ret_decay_logit_fwd, ret_decay_logit_bwd, ret_gn_gain, attn_sink,
           w_branch_ret, w_branch_att, w_out, norm_mix, norm_ffn, w_gate_up, w_down, norm_final):
    batch, seq, d = x.shape
    assert d == D_MODEL and w_in.shape[0] == 1

    w_in0 = w_in[0]
    w_in_b = w_in0.astype(BF16)
    w_q_b = _pair_head_cols(w_in0[:, OFF_AQ:OFF_AK]).astype(BF16)
    nmix = norm_mix[0].reshape(1, D_MODEL)
    nffn = norm_ffn[0].reshape(1, D_MODEL)
    nfin = norm_final.reshape(1, D_MODEL)

    x2d = x.reshape(batch * seq, D_MODEL)
    proj = _inproj_call(x2d, nmix, w_in_b, w_q_b, INPROJ_ROWS)
    proj_meta = _meta_call(meta_tokens.astype(x.dtype), nmix, w_in_b)
    proj3 = proj.reshape(batch, seq, IN_COLS)

    log_gammas = jnp.stack([jax.nn.log_sigmoid(ret_decay_logit_fwd[0].astype(F32)),
                            jax.nn.log_sigmoid(ret_decay_logit_bwd[0].astype(F32))])
    ret = _retention_call(log_gammas, proj3, proj_meta, batch, seq)

    att, wr, wa, wo, wgu, wd = _attention_call(
        attn_sink[0].astype(F32), proj3, proj_meta,
        (w_branch_ret[0], w_branch_att[0], w_out[0], w_gate_up[0], w_down[0]), batch, seq)

    out = _mixffn_call(x2d, ret.reshape(batch * seq, RET_W), ret_gn_gain[0].reshape(1, RET_W),
                       att.reshape(batch * seq, ATT_W), proj,
                       wr, wa, wo, nffn, wgu, wd, nfin)
    return out.reshape(batch, seq, D_MODEL)
```
